```python
import math
import jax, jax.numpy as jnp
from jax import lax
import numpy as np

D_MODEL = 1024
BATCH = 8
SEQ = 4096
DEPTH = 2
DEC_BATCH = 4
DEC_SEQ = 4096
PAST_LEN = 128

N_MEM = 256
MEM_HEADS = 4
MEM_HEAD_DIM = 128
MEM_WIDTH = MEM_HEADS * MEM_HEAD_DIM

MLA_HEADS = 8
MLA_Q_RANK = 384
MLA_KV_RANK = 256
MLA_NOPE = 128
MLA_ROPE = 64
MLA_V = 128
MLA_QK = MLA_NOPE + MLA_ROPE
ROPE_THETA = 10000.0

SWA_HEADS = 16
SWA_KV_HEADS = 2
SWA_HEAD_DIM = 64
SWA_GROUP = SWA_HEADS // SWA_KV_HEADS
WINDOW = 128
BLOCK = 128

MIX_WIDTH = 1024
BRANCH_WIDTH = MIX_WIDTH + MEM_WIDTH
EPS = 1e-6
NEG = -1e30

N_A = (DEPTH + 1) // 2
N_B = DEPTH // 2

IN_A = MLA_Q_RANK + MLA_KV_RANK + MLA_ROPE + MEM_WIDTH + BRANCH_WIDTH
SPLIT_A = (MLA_Q_RANK, MLA_Q_RANK + MLA_KV_RANK, MLA_Q_RANK + MLA_KV_RANK + MLA_ROPE,
           MLA_Q_RANK + MLA_KV_RANK + MLA_ROPE + MEM_WIDTH)
SWA_Q_W = SWA_HEADS * SWA_HEAD_DIM
SWA_KV_W = SWA_KV_HEADS * SWA_HEAD_DIM
IN_B = SWA_Q_W + 2 * SWA_KV_W + MEM_WIDTH + BRANCH_WIDTH
SPLIT_B = (SWA_Q_W, SWA_Q_W + SWA_KV_W, SWA_Q_W + 2 * SWA_KV_W, SWA_Q_W + 2 * SWA_KV_W + MEM_WIDTH)

kernel_name = 'hybrid_mla_swa_memory_encoder'


def rmsnorm(x, g):
    x32 = x.astype(jnp.float32)
    y = x32 * lax.rsqrt(jnp.mean(x32 * x32, axis=-1, keepdims=True) + EPS)
    return (y * g.astype(jnp.float32)).astype(x.dtype)


def rope_tables(seq):
    inv = 1.0 / (ROPE_THETA ** (jnp.arange(0, MLA_ROPE, 2, dtype=jnp.float32) / MLA_ROPE))
    ang = jnp.arange(seq, dtype=jnp.float32)[:, None] * inv[None, :]
    return jnp.cos(ang), jnp.sin(ang)


def apply_rope(x, cos, sin):
    half = x.shape[-1] // 2
    x1, x2 = x[..., :half], x[..., half:]
    c = cos.astype(x.dtype)
    s = sin.astype(x.dtype)
    return jnp.concatenate([x1 * c - x2 * s, x2 * c + x1 * s], axis=-1)


def alibi_slopes(n):
    return 2.0 ** (-8.0 * jnp.arange(1, n + 1, dtype=jnp.float32) / n)


def mla_mixer(c_q, c_kv, k_r, q_a_norm, w_q_b, kv_a_norm, w_kv_b, q_norm, k_norm):
    B, S, _ = c_q.shape
    q = (rmsnorm(c_q, q_a_norm) @ w_q_b).reshape(B, S, MLA_HEADS, MLA_QK)
    kv = (rmsnorm(c_kv, kv_a_norm) @ w_kv_b).reshape(B, S, MLA_HEADS, MLA_NOPE + MLA_V)
    k_nope, v = kv[..., :MLA_NOPE], kv[..., MLA_NOPE:]
    k = jnp.concatenate([k_nope, jnp.broadcast_to(k_r[:, :, None, :], (B, S, MLA_HEADS, MLA_ROPE))], axis=-1)
    q = rmsnorm(q, q_norm)
    k = rmsnorm(k, k_norm)
    cos, sin = rope_tables(S)
    cos, sin = cos[None, :, None, :], sin[None, :, None, :]
    q = jnp.concatenate([q[..., :MLA_NOPE], apply_rope(q[..., MLA_NOPE:], cos, sin)], axis=-1)
    k = jnp.concatenate([k[..., :MLA_NOPE], apply_rope(k[..., MLA_NOPE:], cos, sin)], axis=-1)
    nq = S // BLOCK
    qb = q.reshape(B, nq, BLOCK, MLA_HEADS, MLA_QK).transpose(1, 0, 2, 3, 4)
    scale = MLA_QK ** -0.5

    def attend(q_blk):
        s = jnp.einsum('bqhd,bkhd->bhqk', q_blk, k).astype(jnp.float32) * scale
        p = jax.nn.softmax(s, axis=-1).astype(v.dtype)
        return jnp.einsum('bhqk,bkhd->bqhd', p, v)

    o = lax.map(attend, qb)
    return o.transpose(1, 0, 2, 3, 4).reshape(B, S, MLA_HEADS * MLA_V)


def swa_mixer(q, k, v, q_norm, k_norm, sink):
    B, S, _ = q.shape
    q = rmsnorm(q.reshape(B, S, SWA_HEADS, SWA_HEAD_DIM), q_norm)
    k = rmsnorm(k.reshape(B, S, SWA_KV_HEADS, SWA_HEAD_DIM), k_norm)
    v = v.reshape(B, S, SWA_KV_HEADS, SWA_HEAD_DIM)
    nq = S // BLOCK
    pad = ((0, 0), (BLOCK, BLOCK), (0, 0), (0, 0))
    kp = jnp.pad(k, pad)
    vp = jnp.pad(v, pad)
    qb = q.reshape(B, nq, BLOCK, SWA_KV_HEADS, SWA_GROUP, SWA_HEAD_DIM).transpose(1, 0, 2, 3, 4, 5)
    slopes = alibi_slopes(SWA_HEADS).reshape(SWA_KV_HEADS, SWA_GROUP)[None, :, :, None, None]
    sink_l = sink.astype(jnp.float32).reshape(SWA_KV_HEADS, SWA_GROUP)[None, :, :, None, None]
    rel = jnp.arange(3 * BLOCK)[None, :] - BLOCK - jnp.arange(BLOCK)[:, None]
    dist = jnp.abs(rel).astype(jnp.float32)
    in_window = jnp.abs(rel) <= WINDOW
    scale = SWA_HEAD_DIM ** -0.5

    def attend(args):
        i, q_blk = args
        start = i * BLOCK
        k_blk = lax.dynamic_slice_in_dim(kp, start, 3 * BLOCK, axis=1)
        v_blk = lax.dynamic_slice_in_dim(vp, start, 3 * BLOCK, axis=1)
        key_pos = start - BLOCK + jnp.arange(3 * BLOCK)
        valid = in_window & ((key_pos >= 0) & (key_pos < S))[None, :]
        s = jnp.einsum('bqkgd,bjkd->bkgqj', q_blk, k_blk).astype(jnp.float32) * scale
        s = jnp.where(valid, s - slopes * dist, NEG)
        sk = jnp.broadcast_to(sink_l, s.shape[:-1] + (1,))
        p = jax.nn.softmax(jnp.concatenate([s, sk], axis=-1), axis=-1)[..., :-1]
        return jnp.einsum('bkgqj,bjkd->bqkgd', p.astype(v_blk.dtype), v_blk)

    o = lax.map(attend, (jnp.arange(nq), qb))
    return o.transpose(1, 0, 2, 3, 4, 5).reshape(B, S, SWA_Q_W)


def memory_attention(q, mem, mem_norm, w_mem_kv, q_norm, k_norm):
    B, S, _ = q.shape
    q = rmsnorm(q.reshape(B, S, MEM_HEADS, MEM_HEAD_DIM), q_norm)
    kv = rmsnorm(mem, mem_norm) @ w_mem_kv
    k = rmsnorm(kv[..., :MEM_WIDTH].reshape(B, N_MEM, MEM_HEADS, MEM_HEAD_DIM), k_norm)
    v = kv[..., MEM_WIDTH:].reshape(B, N_MEM, MEM_HEADS, MEM_HEAD_DIM)
    s = jnp.einsum('bshd,bnhd->bhsn', q, k).astype(jnp.float32) * (MEM_HEAD_DIM ** -0.5)
    p = jax.nn.softmax(s, axis=-1).astype(v.dtype)
    return jnp.einsum('bhsn,bnhd->bshd', p, v).reshape(B, S, MEM_WIDTH)


def trunk(x, mem, p):
    for i in range(DEPTH):
        h = rmsnorm(x, p['norm_in'][i])
        j = i // 2
        if i % 2 == 0:
            z = h @ p['a_w_in'][j]
            c_q, c_kv, k_r, q_mem, gate = jnp.split(z, SPLIT_A, axis=-1)
            mix = mla_mixer(c_q, c_kv, k_r, p['a_q_a_norm'][j], p['a_w_q_b'][j],
                            p['a_kv_a_norm'][j], p['a_w_kv_b'][j], p['a_q_norm'][j], p['a_k_norm'][j])
        else:
            z = h @ p['b_w_in'][j]
            q, k, v, q_mem, gate = jnp.split(z, SPLIT_B, axis=-1)
            mix = swa_mixer(q, k, v, p['b_q_norm'][j], p['b_k_norm'][j], p['b_sink'][j])
        mem_o = memory_attention(q_mem, mem, p['mem_norm'][i], p['w_mem_kv'][i],
                                 p['mem_q_norm'][i], p['mem_k_norm'][i])
        branch = jnp.concatenate([mix, mem_o], axis=-1) * jax.nn.silu(gate)
        x = x + branch @ p['w_out'][i]
    return x


def setup_inputs(seed: int = 0) -> dict:
    key = jax.random.key(seed)
    ks = jax.random.split(key, 24)
    f32 = jnp.float32

    def nrm(k, shape, scale):
        return jax.random.normal(k, shape, f32) * scale

    def gain(k, shape):
        return 1.0 + 0.02 * jax.random.normal(k, shape, f32)

    return {
        'x_prompt': nrm(ks[0], (BATCH, SEQ, D_MODEL), 1.0),
        'x_sample': nrm(ks[1], (DEC_BATCH, DEC_SEQ, D_MODEL), 1.0),
        'mem_prompt': nrm(ks[2], (BATCH, N_MEM, D_MODEL), 1.0),
        'mem_sample': nrm(ks[3], (DEC_BATCH, N_MEM, D_MODEL), 1.0),
        'norm_in': gain(ks[4], (DEPTH, D_MODEL)),
        'w_out': nrm(ks[5], (DEPTH, BRANCH_WIDTH, D_MODEL), BRANCH_WIDTH ** -0.5),
        'mem_norm': gain(ks[6], (DEPTH, D_MODEL)),
        'w_mem_kv': nrm(ks[7], (DEPTH, D_MODEL, 2 * MEM_WIDTH), D_MODEL ** -0.5),
        'mem_q_norm': gain(ks[8], (DEPTH, MEM_HEAD_DIM)),
        'mem_k_norm': gain(ks[9], (DEPTH, MEM_HEAD_DIM)),
        'a_w_in': nrm(ks[10], (N_A, D_MODEL, IN_A), D_MODEL ** -0.5),
        'a_q_a_norm': gain(ks[11], (N_A, MLA_Q_RANK)),
        'a_w_q_b': nrm(ks[12], (N_A, MLA_Q_RANK, MLA_HEADS * MLA_QK), MLA_Q_RANK ** -0.5),
        'a_kv_a_norm': gain(ks[13], (N_A, MLA_KV_RANK)),
        'a_w_kv_b': nrm(ks[14], (N_A, MLA_KV_RANK, MLA_HEADS * (MLA_NOPE + MLA_V)), MLA_KV_RANK ** -0.5),
        'a_q_norm': gain(ks[15], (N_A, MLA_QK)),
        'a_k_norm': gain(ks[16], (N_A, MLA_QK)),
        'b_w_in': nrm(ks[17], (N_B, D_MODEL, IN_B), D_MODEL ** -0.5),
        'b_q_norm': gain(ks[18], (N_B, SWA_HEAD_DIM)),
        'b_k_norm': gain(ks[19], (N_B, SWA_HEAD_DIM)),
        'b_sink': nrm(ks[20], (N_B, SWA_HEADS), 0.5),
    }


def reference(x_prompt, x_sample, mem_prompt, mem_sample, norm_in, w_out, mem_norm, w_mem_kv,
              mem_q_norm, mem_k_norm, a_w_in, a_q_a_norm, a_w_q_b, a_kv_a_norm, a_w_kv_b,
              a_q_norm, a_k_norm, b_w_in, b_q_norm, b_k_norm, b_sink):
    p = {
        'norm_in': norm_in, 'w_out': w_out, 'mem_norm': mem_norm, 'w_mem_kv': w_mem_kv,
        'mem_q_norm': mem_q_norm, 'mem_k_norm': mem_k_norm,
        'a_w_in': a_w_in, 'a_q_a_norm': a_q_a_norm, 'a_w_q_b': a_w_q_b, 'a_kv_a_norm': a_kv_a_norm,
        'a_w_kv_b': a_w_kv_b, 'a_q_norm': a_q_norm, 'a_k_norm': a_k_norm,
        'b_w_in': b_w_in, 'b_q_norm': b_q_norm, 'b_k_norm': b_k_norm, 'b_sink': b_sink,
    }
    y_prompt = trunk(x_prompt, mem_prompt, p)
    y_sample = trunk(x_sample, mem_sample, p)
    return (y_prompt, y_sample)
```

```python
import functools
import math

import jax
import jax.numpy as jnp
from jax import lax
from jax.experimental import pallas as pl
from jax.experimental.pallas import tpu as pltpu

D_MODEL = 1024
N_MEM = 256
MEM_HEADS = 4
MEM_HEAD_DIM = 128
MEM_WIDTH = MEM_HEADS * MEM_HEAD_DIM

MLA_HEADS = 8
MLA_Q_RANK = 384
MLA_KV_RANK = 256
MLA_NOPE = 128
MLA_ROPE = 64
MLA_HALF = MLA_ROPE // 2
MLA_V = 128
MLA_QK = MLA_NOPE + MLA_ROPE
ROPE_THETA = 10000.0

SWA_HEADS = 16
SWA_KV_HEADS = 2
SWA_HEAD_DIM = 64
SWA_GROUP = SWA_HEADS // SWA_KV_HEADS
SWA_Q_W = SWA_HEADS * SWA_HEAD_DIM
SWA_KV_W = SWA_KV_HEADS * SWA_HEAD_DIM
WINDOW = 128

MIX_WIDTH = 1024
BRANCH_WIDTH = MIX_WIDTH + MEM_WIDTH
EPS = 1e-6
NEG = -1e30

V7X_LANES = 128
V7X_VMEM_BYTES = 64 * 1024 * 1024

SEQ_TILE = 512
SWA_Q_TILE = 256
SWA_K_BLOCK = 128
SWA_N_KBLOCKS = (SWA_Q_TILE + 2 * WINDOW) // SWA_K_BLOCK

F32 = jnp.float32
BF16 = jnp.bfloat16

_NT = (((1,), (1,)), ((), ()))


def _vmem_limit(nbytes):
    return int(min(nbytes, V7X_VMEM_BYTES - 8 * 1024 * 1024))


def _params(nbytes, ndims):
    return pltpu.CompilerParams(
        dimension_semantics=("arbitrary",) * ndims,
        vmem_limit_bytes=_vmem_limit(nbytes))


def _dot(a, b):
    return jnp.dot(a, b, preferred_element_type=F32)


def _dot_nt(a, b):
    return lax.dot_general(a, b, _NT, preferred_element_type=F32)


def _rms_lanes(x):
    return x * lax.rsqrt(jnp.mean(x * x, axis=-1, keepdims=True) + EPS)


def _rms_scale_rows(x):
    return lax.rsqrt(jnp.mean(x * x, axis=0, keepdims=True) + EPS)


def _full(shape):
    return pl.BlockSpec(shape, lambda *_: (0,) * len(shape))


def _memkv_kernel(mem_ref, g_ref, wk_ref, wvt_ref, gk_ref, km_ref, vmt_ref):
    mn = (_rms_lanes(mem_ref[0]) * g_ref[...]).astype(BF16)
    k = _dot(mn, wk_ref[...])
    vt = _dot_nt(wvt_ref[...], mn)
    for h in range(MEM_HEADS):
        lo = h * MEM_HEAD_DIM
        kh = k[:, lo:lo + MEM_HEAD_DIM]
        km_ref[0, h] = (_rms_lanes(kh) * gk_ref[...]).astype(BF16)
        vmt_ref[0, h] = vt[lo:lo + MEM_HEAD_DIM].astype(BF16)


def _memkv(mem, g, wk, wvt, gk):
    b = mem.shape[0]
    return pl.pallas_call(
        _memkv_kernel,
        grid=(b,),
        in_specs=[
            pl.BlockSpec((1, N_MEM, D_MODEL), lambda i: (i, 0, 0)),
            _full((1, D_MODEL)),
            _full((D_MODEL, MEM_WIDTH)),
            _full((MEM_WIDTH, D_MODEL)),
            _full((1, MEM_HEAD_DIM)),
        ],
        out_specs=[
            pl.BlockSpec((1, MEM_HEADS, N_MEM, MEM_HEAD_DIM), lambda i: (i, 0, 0, 0)),
            pl.BlockSpec((1, MEM_HEADS, MEM_HEAD_DIM, N_MEM), lambda i: (i, 0, 0, 0)),
        ],
        out_shape=[
            jax.ShapeDtypeStruct((b, MEM_HEADS, N_MEM, MEM_HEAD_DIM), BF16),
            jax.ShapeDtypeStruct((b, MEM_HEADS, MEM_HEAD_DIM, N_MEM), BF16),
        ],
        compiler_params=_params(24 * 2**20, 1),
        name="memkv",
    )(mem, g, wk, wvt, gk)


def _normed_input(x_ref, gin_ref):
    return (_rms_lanes(x_ref[0]) * gin_ref[...]).astype(BF16)


def _mem_query_and_gate(zt, gqm_ref, qmt_ref, sgt_ref):
    for h in range(MEM_HEADS):
        lo = h * MEM_HEAD_DIM
        qm = zt[lo:lo + MEM_HEAD_DIM]
        qmt_ref[0, lo:lo + MEM_HEAD_DIM, :] = (qm * gqm_ref[...] * _rms_scale_rows(qm)).astype(BF16)
    g = zt[MEM_WIDTH:]
    sgt_ref[0] = (g * jax.nn.sigmoid(g)).astype(BF16)


def _proj_a_kernel(x_ref, gin_ref, w1_ref, wgt_ref, gqa_ref, wqbt_ref, gkva_ref, wkn_ref, wvt_ref,
                   gqn_ref, qtab_ref, gkn_ref, gkr_ref, ctab_ref, stab_ref, gqm_ref,
                   qt_ref, k_ref, vt_ref, qmt_ref, sgt_ref):
    h = _normed_input(x_ref, gin_ref)
    z1 = _dot(h, w1_ref[...])
    kv_lo = MLA_Q_RANK
    kr_lo = MLA_Q_RANK + MLA_KV_RANK
    cqn = (_rms_lanes(z1[:, :kv_lo]) * gqa_ref[...]).astype(BF16)
    ckvn = (_rms_lanes(z1[:, kv_lo:kr_lo]) * gkva_ref[...]).astype(BF16)
    kr = z1[:, kr_lo:]

    qt = _dot_nt(wqbt_ref[...], cqn)
    a1 = qtab_ref[0 * MLA_HALF:1 * MLA_HALF]
    b1 = qtab_ref[1 * MLA_HALF:2 * MLA_HALF]
    a2 = qtab_ref[2 * MLA_HALF:3 * MLA_HALF]
    b2 = qtab_ref[3 * MLA_HALF:4 * MLA_HALF]
    for hh in range(MLA_HEADS):
        qh = qt[hh * MLA_QK:(hh + 1) * MLA_QK]
        r = _rms_scale_rows(qh)
        x1 = qh[MLA_NOPE:MLA_NOPE + MLA_HALF]
        x2 = qh[MLA_NOPE + MLA_HALF:]
        qt_ref[0, hh, 0, :MLA_NOPE, :] = (qh[:MLA_NOPE] * gqn_ref[...] * r).astype(BF16)
        qt_ref[0, hh, 0, MLA_NOPE:MLA_NOPE + MLA_HALF, :] = ((x1 * a1 - x2 * b1) * r).astype(BF16)
        qt_ref[0, hh, 0, MLA_NOPE + MLA_HALF:, :] = ((x2 * a2 + x1 * b2) * r).astype(BF16)

    kn = _dot(ckvn, wkn_ref[...])
    vt = _dot_nt(wvt_ref[...], ckvn)
    krg = kr * gkr_ref[...]
    kro = krg * ctab_ref[...] + pltpu.roll(krg, MLA_HALF, 1) * stab_ref[...]
    ss_r = 0.5 * jnp.sum(kr * kr, axis=1, keepdims=True)
    for hh in range(MLA_HEADS):
        knh = kn[:, hh * MLA_NOPE:(hh + 1) * MLA_NOPE]
        ss = jnp.sum(knh * knh, axis=1, keepdims=True) + ss_r
        r = lax.rsqrt(ss * (1.0 / MLA_QK) + EPS)
        k_ref[0, hh, :, :MLA_NOPE] = (knh * gkn_ref[...] * r).astype(BF16)
        k_ref[0, hh, :, MLA_NOPE:] = (kro[:, :MLA_ROPE] * r).astype(BF16)
        vt_ref[0, hh, 0] = vt[hh * MLA_V:(hh + 1) * MLA_V].astype(BF16)

    _mem_query_and_gate(_dot_nt(wgt_ref[...], h), gqm_ref, qmt_ref, sgt_ref)


def _proj_a(x, wa, ts):
    b, s, _ = x.shape
    nt = s // ts
    consts = [wa["gin"], wa["w1"], wa["wgt"], wa["gqa"], wa["wqbt"], wa["gkva"], wa["wkn"], wa["wvt"], wa["gqn"]]
    tail = [wa["gkn"], wa["gkr"]]
    in_specs = (
        [pl.BlockSpec((1, ts, D_MODEL), lambda i, j: (i, j, 0))]
        + [_full(c.shape) for c in consts]
        + [pl.BlockSpec((4 * MLA_HALF, ts), lambda i, j: (0, j))]
        + [_full(c.shape) for c in tail]
        + [pl.BlockSpec((ts, V7X_LANES), lambda i, j: (j, 0))] * 2
        + [_full(wa["gqm"].shape)]
    )
    out_shape = [
        jax.ShapeDtypeStruct((b, MLA_HEADS, nt, MLA_QK, ts), BF16),
        jax.ShapeDtypeStruct((b, MLA_HEADS, s, MLA_QK), BF16),
        jax.ShapeDtypeStruct((b, MLA_HEADS, nt, MLA_V, ts), BF16),
        jax.ShapeDtypeStruct((b, MEM_WIDTH, s), BF16),
        jax.ShapeDtypeStruct((b, BRANCH_WIDTH, s), BF16),
    ]
    out_specs = [
        pl.BlockSpec((1, MLA_HEADS, 1, MLA_QK, ts), lambda i, j: (i, 0, j, 0, 0)),
        pl.BlockSpec((1, MLA_HEADS, ts, MLA_QK), lambda i, j: (i, 0, j, 0)),
        pl.BlockSpec((1, MLA_HEADS, 1, MLA_V, ts), lambda i, j: (i, 0, j, 0, 0)),
        pl.BlockSpec((1, MEM_WIDTH, ts), lambda i, j: (i, 0, j)),
        pl.BlockSpec((1, BRANCH_WIDTH, ts), lambda i, j: (i, 0, j)),
    ]
    return pl.pallas_call(
        _proj_a_kernel,
        grid=(b, nt),
        in_specs=in_specs,
        out_specs=out_specs,
        out_shape=out_shape,
        compiler_params=_params(56 * 2**20, 2),
        name="proj_a",
    )(x, *consts, wa["qtab"], *tail, wa["ctab"], wa["stab"], wa["gqm"])


def _mla_attn_kernel(qt_ref, k_ref, vt_ref, o_ref, m_sc, l_sc, acc_sc):
    nq = qt_ref.shape[2]
    nk = vt_ref.shape[2]
    tk = vt_ref.shape[4]

    def q_tile(qi, carry):
        m_sc[...] = jnp.full(m_sc.shape, NEG, F32)
        l_sc[...] = jnp.zeros(l_sc.shape, F32)
        acc_sc[...] = jnp.zeros(acc_sc.shape, F32)

        def k_chunk(kj, c):
            k = k_ref[0, 0, pl.ds(pl.multiple_of(kj * tk, tk), tk), :]
            st = _dot(k, qt_ref[0, 0, qi])
            m_prev = m_sc[...]
            m_new = jnp.maximum(m_prev, jnp.max(st, axis=0, keepdims=True))
            alpha = jnp.exp(m_prev - m_new)
            p = jnp.exp(st - m_new)
            l_sc[...] = alpha * l_sc[...] + jnp.sum(p, axis=0, keepdims=True)
            acc_sc[...] = alpha * acc_sc[...] + _dot(vt_ref[0, 0, kj], p.astype(BF16))
            m_sc[...] = m_new
            return c

        lax.fori_loop(0, nk, k_chunk, 0)
        o_ref[0, 0, qi] = (acc_sc[...] / l_sc[...]).astype(BF16)
        return carry

    lax.fori_loop(0, nq, q_tile, 0)


def _mla_attn(qt, k, vt):
    b, nh, nt, _, ts = qt.shape
    s = k.shape[2]
    return pl.pallas_call(
        _mla_attn_kernel,
        grid=(b, nh),
        in_specs=[
            pl.BlockSpec((1, 1, nt, MLA_QK, ts), lambda i, j: (i, j, 0, 0, 0)),
            pl.BlockSpec((1, 1, s, MLA_QK), lambda i, j: (i, j, 0, 0)),
            pl.BlockSpec((1, 1, nt, MLA_V, ts), lambda i, j: (i, j, 0, 0, 0)),
        ],
        out_specs=pl.BlockSpec((1, 1, nt, MLA_V, ts), lambda i, j: (i, j, 0, 0, 0)),
        out_shape=jax.ShapeDtypeStruct((b, nh, nt, MLA_V, ts), BF16),
        scratch_shapes=[
            pltpu.VMEM((1, ts), F32),
            pltpu.VMEM((1, ts), F32),
            pltpu.VMEM((MLA_V, ts), F32),
        ],
        compiler_params=_params(40 * 2**20, 2),
        name="mla_attn",
    )(qt, k, vt)


def _post_kernel(x_ref, mixt_ref, qmt_ref, sgt_ref, km_ref, vmt_ref, wot_ref, o_ref):
    ts = x_ref.shape[1]
    mixt = mixt_ref[...].reshape(MIX_WIDTH, ts)
    memo = []
    for h in range(MEM_HEADS):
        lo = h * MEM_HEAD_DIM
        st = _dot(km_ref[0, h], qmt_ref[0, lo:lo + MEM_HEAD_DIM, :])
        p = jnp.exp(st - jnp.max(st, axis=0, keepdims=True))
        l = jnp.sum(p, axis=0, keepdims=True)
        memo.append(_dot(vmt_ref[0, h], p.astype(BF16)) / l)
    sg = sgt_ref[0]
    bmix = (mixt.astype(F32) * sg[:MIX_WIDTH].astype(F32)).astype(BF16)
    bmem = (jnp.concatenate(memo, axis=0) * sg[MIX_WIDTH:].astype(F32)).astype(BF16)
    outt = _dot(wot_ref[:, :MIX_WIDTH], bmix) + _dot(wot_ref[:, MIX_WIDTH:], bmem)
    o_ref[0] = x_ref[0] + outt.T


def _post(x, mixt, mixt_spec, qmt, sgt, km, vmt, wot, ts):
    b, s, _ = x.shape
    return pl.pallas_call(
        _post_kernel,
        grid=(b, s // ts),
        in_specs=[
            pl.BlockSpec((1, ts, D_MODEL), lambda i, j: (i, j, 0)),
            mixt_spec,
            pl.BlockSpec((1, MEM_WIDTH, ts), lambda i, j: (i, 0, j)),
            pl.BlockSpec((1, BRANCH_WIDTH, ts), lambda i, j: (i, 0, j)),
            pl.BlockSpec((1, MEM_HEADS, N_MEM, MEM_HEAD_DIM), lambda i, j: (i, 0, 0, 0)),
            pl.BlockSpec((1, MEM_HEADS, MEM_HEAD_DIM, N_MEM), lambda i, j: (i, 0, 0, 0)),
            _full(wot.shape),
        ],
        out_specs=pl.BlockSpec((1, ts, D_MODEL), lambda i, j: (i, j, 0)),
        out_shape=jax.ShapeDtypeStruct(x.shape, x.dtype),
        compiler_params=_params(48 * 2**20, 2),
        name="post",
    )(x, mixt, qmt, sgt, km, vmt, wot)


def _proj_b_kernel(x_ref, gin_ref, wbt_ref, gq_ref, gk_ref, gqm_ref,
                   qt_ref, k_ref, vt_ref, qmt_ref, sgt_ref):
    h = _normed_input(x_ref, gin_ref)
    qt = _dot_nt(wbt_ref[:SWA_Q_W], h)
    for hh in range(SWA_HEADS):
        lo = hh * SWA_HEAD_DIM
        qh = qt[lo:lo + SWA_HEAD_DIM]
        qt_ref[0, lo:lo + SWA_HEAD_DIM, :] = (qh * gq_ref[...] * _rms_scale_rows(qh)).astype(BF16)
    kvt = _dot_nt(wbt_ref[SWA_Q_W:SWA_Q_W + 2 * SWA_KV_W], h)
    kparts = []
    for g in range(SWA_KV_HEADS):
        kh = kvt[g * SWA_HEAD_DIM:(g + 1) * SWA_HEAD_DIM]
        kparts.append(kh * gk_ref[...] * _rms_scale_rows(kh))
    k_ref[0] = jnp.concatenate(kparts, axis=0).T.astype(BF16)
    vt_ref[0] = kvt[SWA_KV_W:].astype(BF16)
    _mem_query_and_gate(_dot_nt(wbt_ref[SWA_Q_W + 2 * SWA_KV_W:], h), gqm_ref, qmt_ref, sgt_ref)


def _proj_b(x, wb, ts):
    b, s, _ = x.shape
    consts = [wb["gin"], wb["wbt"], wb["gq"], wb["gk"], wb["gqm"]]
    return pl.pallas_call(
        _proj_b_kernel,
        grid=(b, s // ts),
        in_specs=[pl.BlockSpec((1, ts, D_MODEL), lambda i, j: (i, j, 0))] + [_full(c.shape) for c in consts],
        out_specs=[
            pl.BlockSpec((1, SWA_Q_W, ts), lambda i, j: (i, 0, j)),
            pl.BlockSpec((1, ts, SWA_KV_W), lambda i, j: (i, j, 0)),
            pl.BlockSpec((1, SWA_KV_W, ts), lambda i, j: (i, 0, j)),
            pl.BlockSpec((1, MEM_WIDTH, ts), lambda i, j: (i, 0, j)),
            pl.BlockSpec((1, BRANCH_WIDTH, ts), lambda i, j: (i, 0, j)),
        ],
        out_shape=[
            jax.ShapeDtypeStruct((b, SWA_Q_W, s), BF16),
            jax.ShapeDtypeStruct((b, s, SWA_KV_W), BF16),
            jax.ShapeDtypeStruct((b, SWA_KV_W, s), BF16),
            jax.ShapeDtypeStruct((b, MEM_WIDTH, s), BF16),
            jax.ShapeDtypeStruct((b, BRANCH_WIDTH, s), BF16),
        ],
        compiler_params=_params(56 * 2**20, 2),
        name="proj_b",
    )(x, *consts)


def _alibi_slope(h):
    return 2.0 ** (-8.0 * (h + 1) / SWA_HEADS)


def _swa_kernel(sink_ref, qt_ref, k0_ref, k1_ref, k2_ref, k3_ref, v0_ref, v1_ref, v2_ref, v3_ref,
                t_ref, o_ref):
    i = pl.program_id(1)
    n_kblocks = pl.num_programs(1) * (SWA_Q_TILE // SWA_K_BLOCK)
    k_all = jnp.concatenate([k0_ref[0], k1_ref[0], k2_ref[0], k3_ref[0]], axis=0)
    vt_all = jnp.concatenate([v0_ref[0], v1_ref[0], v2_ref[0], v3_ref[0]], axis=1)
    nkeys = k_all.shape[0]
    row = lax.broadcasted_iota(jnp.int32, (nkeys, 1), 0)
    key_block = i * (SWA_Q_TILE // SWA_K_BLOCK) - 1 + row // SWA_K_BLOCK
    valid = (key_block >= 0) & (key_block < n_kblocks)
    t = t_ref[...]
    for g in range(SWA_KV_HEADS):
        k_g = k_all[:, g * SWA_HEAD_DIM:(g + 1) * SWA_HEAD_DIM]
        vt_g = vt_all[g * SWA_HEAD_DIM:(g + 1) * SWA_HEAD_DIM]
        for hq in range(SWA_GROUP):
            h = g * SWA_GROUP + hq
            lo = h * SWA_HEAD_DIM
            st = _dot(k_g, qt_ref[0, lo:lo + SWA_HEAD_DIM, :])
            st = jnp.where(valid, st + _alibi_slope(h) * t, NEG)
            sink = sink_ref[h]
            m = jnp.maximum(jnp.max(st, axis=0, keepdims=True), sink)
            p = jnp.exp(st - m)
            l = jnp.sum(p, axis=0, keepdims=True) + jnp.exp(sink - m)
            ot = _dot(vt_g, p.astype(BF16)) / l
            o_ref[0, lo:lo + SWA_HEAD_DIM, :] = ot.astype(BF16)


def _swa_attn(qt, k, vt, sink, ttab):
    b, _, s = qt.shape
    tq = SWA_Q_TILE
    per_tile = tq // SWA_K_BLOCK
    last = s // SWA_K_BLOCK - 1

    def kblock(o):
        return lambda i, j, *_: (i, jnp.clip(j * per_tile - 1 + o, 0, last), 0)

    def vblock(o):
        return lambda i, j, *_: (i, 0, jnp.clip(j * per_tile - 1 + o, 0, last))

    grid_spec = pltpu.PrefetchScalarGridSpec(
        num_scalar_prefetch=1,
        grid=(b, s // tq),
        in_specs=(
            [pl.BlockSpec((1, SWA_Q_W, tq), lambda i, j, *_: (i, 0, j))]
            + [pl.BlockSpec((1, SWA_K_BLOCK, SWA_KV_W), kblock(o)) for o in range(SWA_N_KBLOCKS)]
            + [pl.BlockSpec((1, SWA_KV_W, SWA_K_BLOCK), vblock(o)) for o in range(SWA_N_KBLOCKS)]
            + [pl.BlockSpec(ttab.shape, lambda i, j, *_: (0, 0))]
        ),
        out_specs=pl.BlockSpec((1, SWA_Q_W, tq), lambda i, j, *_: (i, 0, j)),
    )
    return pl.pallas_call(
        _swa_kernel,
        grid_spec=grid_spec,
        out_shape=jax.ShapeDtypeStruct((b, SWA_Q_W, s), BF16),
        compiler_params=_params(32 * 2**20, 2),
        name="swa_attn",
    )(sink, qt, k, k, k, k, vt, vt, vt, vt, ttab)


def _col(v):
    return v.astype(F32)[:, None]


def _row(v):
    return v.astype(F32)[None, :]


def _prep_layer_a(seq, norm_in, a_w_in, a_q_a_norm, a_w_q_b, a_kv_a_norm, a_w_kv_b, a_q_norm, a_k_norm, mem_q_norm):
    kr_lo = MLA_Q_RANK + MLA_KV_RANK
    qm_lo = kr_lo + MLA_ROPE
    kr_cols = a_w_in[:, kr_lo:qm_lo]
    w1 = jnp.concatenate([a_w_in[:, :kr_lo], kr_cols, kr_cols], axis=1).astype(BF16)
    wkv = a_w_kv_b.reshape(MLA_KV_RANK, MLA_HEADS, MLA_NOPE + MLA_V)
    inv = 1.0 / (ROPE_THETA ** (jnp.arange(0, MLA_ROPE, 2, dtype=F32) / MLA_ROPE))
    ang = jnp.arange(seq, dtype=F32)[:, None] * inv[None, :]
    cos, sin = jnp.cos(ang), jnp.sin(ang)
    scale = MLA_QK ** -0.5
    g1 = _col(a_q_norm[MLA_NOPE:MLA_NOPE + MLA_HALF]) * scale
    g2 = _col(a_q_norm[MLA_NOPE + MLA_HALF:]) * scale
    qtab = jnp.concatenate([g1 * cos.T, g2 * sin.T, g2 * cos.T, g1 * sin.T], axis=0)
    return {
        "gin": _row(norm_in),
        "w1": w1,
        "wgt": a_w_in[:, qm_lo:].T.astype(BF16),
        "gqa": _row(a_q_a_norm),
        "wqbt": a_w_q_b.T.astype(BF16),
        "gkva": _row(a_kv_a_norm),
        "wkn": wkv[:, :, :MLA_NOPE].reshape(MLA_KV_RANK, MLA_HEADS * MLA_NOPE).astype(BF16),
        "wvt": wkv[:, :, MLA_NOPE:].reshape(MLA_KV_RANK, MLA_HEADS * MLA_V).T.astype(BF16),
        "gqn": _col(a_q_norm[:MLA_NOPE]) * scale,
        "qtab": qtab,
        "gkn": _row(a_k_norm[:MLA_NOPE]),
        "gkr": _row(jnp.tile(a_k_norm[MLA_NOPE:], 2)),
        "ctab": jnp.tile(cos, (1, 4)),
        "stab": jnp.concatenate([-sin, sin, -sin, sin], axis=1),
        "gqm": _col(mem_q_norm) * (MEM_HEAD_DIM ** -0.5),
    }


def _prep_layer_b(norm_in, b_w_in, b_q_norm, b_k_norm, mem_q_norm):
    return {
        "gin": _row(norm_in),
        "wbt": b_w_in.T.astype(BF16),
        "gq": _col(b_q_norm) * (SWA_HEAD_DIM ** -0.5),
        "gk": _col(b_k_norm),
        "gqm": _col(mem_q_norm) * (MEM_HEAD_DIM ** -0.5),
    }


def _prep_mem(mem_norm, w_mem_kv, mem_k_norm):
    return {
        "g": _row(mem_norm),
        "wk": w_mem_kv[:, :MEM_WIDTH].astype(BF16),
        "wvt": w_mem_kv[:, MEM_WIDTH:].T.astype(BF16),
        "gk": _row(mem_k_norm),
    }


def _swa_table():
    r = jnp.arange(SWA_N_KBLOCKS * SWA_K_BLOCK)[:, None]
    c = jnp.arange(SWA_Q_TILE)[None, :]
    rel = r - WINDOW - c
    return jnp.where(jnp.abs(rel) <= WINDOW, -jnp.abs(rel).astype(F32), NEG)


def _trunk(x, mem, wa, wb, wmem, wots, sink, ttab):
    ts = min(SEQ_TILE, x.shape[1])
    b = x.shape[0]
    nt = x.shape[1] // ts

    km, vmt = _memkv(mem, **wmem[0])
    qt, k, vt, qmt, sgt = _proj_a(x, wa, ts)
    mixt = _mla_attn(qt, k, vt)
    mix_spec = pl.BlockSpec((1, MLA_HEADS, 1, MLA_V, ts), lambda i, j: (i, 0, j, 0, 0))
    x = _post(x, mixt, mix_spec, qmt, sgt, km, vmt, wots[0], ts)

    km, vmt = _memkv(mem, **wmem[1])
    qt, k, vt, qmt, sgt = _proj_b(x, wb, ts)
    mixt = _swa_attn(qt, k, vt, sink, ttab)
    mix_spec = pl.BlockSpec((1, MIX_WIDTH, ts), lambda i, j: (i, 0, j))
    return _post(x, mixt, mix_spec, qmt, sgt, km, vmt, wots[1], ts)


def kernel(x_prompt, x_sample, mem_prompt, mem_sample, norm_in, w_out, mem_norm, w_mem_kv, mem_q_norm, mem_k_norm, a_w_in, a_q_a_norm, a_w_q_b, a_kv_a_norm, a_w_kv_b, a_q_norm, a_k_norm, b_w_in, b_q_norm, b_k_norm, b_sink):
    assert norm_in.shape[0] == 2 and a_w_in.shape[0] == 1 and b_w_in.shape[0] == 1
    assert x_prompt.shape[1] == x_sample.shape[1]
    seq = x_prompt.shape[1]
    assert seq % SWA_Q_TILE == 0 and seq % min(SEQ_TILE, seq) == 0
    wa = _prep_layer_a(seq, norm_in[0], a_w_in[0], a_q_a_norm[0], a_w_q_b[0], a_kv_a_norm[0], a_w_kv_b[0],
                       a_q_norm[0], a_k_norm[0], mem_q_norm[0])
    wb = _prep_layer_b(norm_in[1], b_w_in[0], b_q_norm[0], b_k_norm[0], mem_q_norm[1])
    wmem = [_prep_mem(mem_norm[i], w_mem_kv[i], mem_k_norm[i]) for i in range(2)]
    wots = [w_out[i].T.astype(BF16) for i in range(2)]
    sink = b_sink[0].astype(F32)
    ttab = _swa_table()
    y_prompt = _trunk(x_prompt, mem_prompt, wa, wb, wmem, wots, sink, ttab)
    y_sample = _trunk(x_sample, mem_sample, wa, wb, wmem, wots, sink, ttab)
    return (y_prompt, y_sample)
```

```python
import functools
import math

import jax
import jax.numpy as jnp
from jax import lax
from jax.experimental import pallas as pl
from jax.experimental.pallas import tpu as pltpu

D_MODEL = 1024
N_MEM = 256
MEM_HEADS = 4
MEM_HEAD_DIM = 128
MEM_WIDTH = MEM_HEADS * MEM_HEAD_DIM

MLA_HEADS = 8
MLA_Q_RANK = 384
MLA_KV_RANK = 256
MLA_NOPE = 128
MLA_ROPE = 64
MLA_HALF = MLA_ROPE // 2
MLA_V = 128
MLA_QK = MLA_NOPE + MLA_ROPE
ROPE_THETA = 10000.0

SWA_HEADS = 16
SWA_KV_HEADS = 2
SWA_HEAD_DIM = 64
SWA_GROUP = SWA_HEADS // SWA_KV_HEADS
SWA_Q_W = SWA_HEADS * SWA_HEAD_DIM
SWA_KV_W = SWA_KV_HEADS * SWA_HEAD_DIM
WINDOW = 128

MIX_WIDTH = 1024
BRANCH_WIDTH = MIX_WIDTH + MEM_WIDTH
EPS = 1e-6
NEG = -1e30

V7X_LANES = 128
V7X_VMEM_BYTES = 64 * 1024 * 1024

SEQ_TILE = 512
SWA_Q_TILE = 256
SWA_K_BLOCK = 128
SWA_N_KBLOCKS = (SWA_Q_TILE + 2 * WINDOW) // SWA_K_BLOCK

F32 = jnp.float32
BF16 = jnp.bfloat16

_NT = (((1,), (1,)), ((), ()))


def _vmem_limit(nbytes):
    return int(min(nbytes, V7X_VMEM_BYTES - 8 * 1024 * 1024))


def _params(nbytes, ndims):
    return pltpu.CompilerParams(
        dimension_semantics=("arbitrary",) * ndims,
        vmem_limit_bytes=_vmem_limit(nbytes))


def _dot(a, b):
    return jnp.dot(a, b, preferred_element_type=F32)


def _dot_nt(a, b):
    return lax.dot_general(a, b, _NT, preferred_element_type=F32)


def _rms_lanes(x):
    return x * lax.rsqrt(jnp.mean(x * x, axis=-1, keepdims=True) + EPS)


def _rms_scale_rows(x):
    return lax.rsqrt(jnp.mean(x * x, axis=0, keepdims=True) + EPS)


def _full(shape):
    return pl.BlockSpec(shape, lambda *_: (0,) * len(shape))


def _memkv_kernel(mem_ref, g_ref, wk_ref, wvt_ref, gk_ref, km_ref, vmt_ref):
    mn = (_rms_lanes(mem_ref[0]) * g_ref[...]).astype(BF16)
    k = _dot(mn, wk_ref[...])
    vt = _dot_nt(wvt_ref[...], mn)
    for h in range(MEM_HEADS):
        lo = h * MEM_HEAD_DIM
        kh = k[:, lo:lo + MEM_HEAD_DIM]
        km_ref[0, h] = (_rms_lanes(kh) * gk_ref[...]).astype(BF16)
        vmt_ref[0, h] = vt[lo:lo + MEM_HEAD_DIM].astype(BF16)


def _memkv(mem, g, wk, wvt, gk):
    b = mem.shape[0]
    return pl.pallas_call(
        _memkv_kernel,
        grid=(b,),
        in_specs=[
            pl.BlockSpec((1, N_MEM, D_MODEL), lambda i: (i, 0, 0)),
            _full((1, D_MODEL)),
            _full((D_MODEL, MEM_WIDTH)),
            _full((MEM_WIDTH, D_MODEL)),
            _full((1, MEM_HEAD_DIM)),
        ],
        out_specs=[
            pl.BlockSpec((1, MEM_HEADS, N_MEM, MEM_HEAD_DIM), lambda i: (i, 0, 0, 0)),
            pl.BlockSpec((1, MEM_HEADS, MEM_HEAD_DIM, N_MEM), lambda i: (i, 0, 0, 0)),
        ],
        out_shape=[
            jax.ShapeDtypeStruct((b, MEM_HEADS, N_MEM, MEM_HEAD_DIM), BF16),
            jax.ShapeDtypeStruct((b, MEM_HEADS, MEM_HEAD_DIM, N_MEM), BF16),
        ],
        compiler_params=_params(24 * 2**20, 1),
        name="memkv",
    )(mem, g, wk, wvt, gk)


def _normed_input(x_ref, gin_ref):
    return (_rms_lanes(x_ref[0]) * gin_ref[...]).astype(BF16)


def _mem_query_and_gate(zt, gqm_ref, qmt_ref, sgt_ref):
    for h in range(MEM_HEADS):
        lo = h * MEM_HEAD_DIM
        qm = zt[lo:lo + MEM_HEAD_DIM]
        qmt_ref[0, lo:lo + MEM_HEAD_DIM, :] = (qm * gqm_ref[...] * _rms_scale_rows(qm)).astype(BF16)
    g = zt[MEM_WIDTH:]
    sgt_ref[0] = (g * jax.nn.sigmoid(g)).astype(BF16)


def _proj_a_kernel(x_ref, gin_ref, w1_ref, wgt_ref, gqa_ref, wqbt_ref, gkva_ref, wkn_ref, wvt_ref,
                   gqn_ref, qtab_ref, gkn_ref, gkr_ref, ctab_ref, stab_ref, gqm_ref,
                   qt_ref, k_ref, vt_ref, qmt_ref, sgt_ref):
    h = _normed_input(x_ref, gin_ref)
    z1 = _dot(h, w1_ref[...])
    kv_lo = MLA_Q_RANK
    kr_lo = MLA_Q_RANK + MLA_KV_RANK
    cqn = (_rms_lanes(z1[:, :kv_lo]) * gqa_ref[...]).astype(BF16)
    ckvn = (_rms_lanes(z1[:, kv_lo:kr_lo]) * gkva_ref[...]).astype(BF16)
    kr = z1[:, kr_lo:]

    qt = _dot_nt(wqbt_ref[...], cqn)
    a1 = qtab_ref[0 * MLA_HALF:1 * MLA_HALF]
    b1 = qtab_ref[1 * MLA_HALF:2 * MLA_HALF]
    a2 = qtab_ref[2 * MLA_HALF:3 * MLA_HALF]
    b2 = qtab_ref[3 * MLA_HALF:4 * MLA_HALF]
    for hh in range(MLA_HEADS):
        qh = qt[hh * MLA_QK:(hh + 1) * MLA_QK]
        r = _rms_scale_rows(qh)
        x1 = qh[MLA_NOPE:MLA_NOPE + MLA_HALF]
        x2 = qh[MLA_NOPE + MLA_HALF:]
        qt_ref[0, hh, 0, :MLA_NOPE, :] = (qh[:MLA_NOPE] * gqn_ref[...] * r).astype(BF16)
        qt_ref[0, hh, 0, MLA_NOPE:MLA_NOPE + MLA_HALF, :] = ((x1 * a1 - x2 * b1) * r).astype(BF16)
        qt_ref[0, hh, 0, MLA_NOPE + MLA_HALF:, :] = ((x2 * a2 + x1 * b2) * r).astype(BF16)

    kn = _dot(ckvn, wkn_ref[...])
    vt = _dot_nt(wvt_ref[...], ckvn)
    krg = kr * gkr_ref[...]
    kro = krg * ctab_ref[...] + pltpu.roll(krg, MLA_HALF, 1) * stab_ref[...]
    ss_r = 0.5 * jnp.sum(kr * kr, axis=1, keepdims=True)
    for hh in range(MLA_HEADS):
        knh = kn[:, hh * MLA_NOPE:(hh + 1) * MLA_NOPE]
        ss = jnp.sum(knh * knh, axis=1, keepdims=True) + ss_r
        r = lax.rsqrt(ss * (1.0 / MLA_QK) + EPS)
        k_ref[0, hh, :, :MLA_NOPE] = (knh * gkn_ref[...] * r).astype(BF16)
        k_ref[0, hh, :, MLA_NOPE:] = (kro[:, :MLA_ROPE] * r).astype(BF16)
        vt_ref[0, hh, 0] = vt[hh * MLA_V:(hh + 1) * MLA_V].astype(BF16)

    _mem_query_and_gate(_dot_nt(wgt_ref[...], h), gqm_ref, qmt_ref, sgt_ref)


def _proj_a(x, wa, ts):
    b, s, _ = x.shape
    nt = s // ts
    consts = [wa["gin"], wa["w1"], wa["wgt"], wa["gqa"], wa["wqbt"], wa["gkva"], wa["wkn"], wa["wvt"], wa["gqn"]]
    tail = [wa["gkn"], wa["gkr"]]
    in_specs = (
        [pl.BlockSpec((1, ts, D_MODEL), lambda i, j: (i, j, 0))]
        + [_full(c.shape) for c in consts]
        + [pl.BlockSpec((4 * MLA_HALF, ts), lambda i, j: (0, j))]
        + [_full(c.shape) for c in tail]
        + [pl.BlockSpec((ts, V7X_LANES), lambda i, j: (j, 0))] * 2
        + [_full(wa["gqm"].shape)]
    )
    out_shape = [
        jax.ShapeDtypeStruct((b, MLA_HEADS, nt, MLA_QK, ts), BF16),
        jax.ShapeDtypeStruct((b, MLA_HEADS, s, MLA_QK), BF16),
        jax.ShapeDtypeStruct((b, MLA_HEADS, nt, MLA_V, ts), BF16),
        jax.ShapeDtypeStruct((b, MEM_WIDTH, s), BF16),
        jax.ShapeDtypeStruct((b, BRANCH_WIDTH, s), BF16),
    ]
    out_specs = [
        pl.BlockSpec((1, MLA_HEADS, 1, MLA_QK, ts), lambda i, j: (i, 0, j, 0, 0)),
        pl.BlockSpec((1, MLA_HEADS, ts, MLA_QK), lambda i, j: (i, 0, j, 0)),
        pl.BlockSpec((1, MLA_HEADS, 1, MLA_V, ts), lambda i, j: (i, 0, j, 0, 0)),
        pl.BlockSpec((1, MEM_WIDTH, ts), lambda i, j: (i, 0, j)),
        pl.BlockSpec((1, BRANCH_WIDTH, ts), lambda i, j: (i, 0, j)),
    ]
    return pl.pallas_call(
        _proj_a_kernel,
        grid=(b, nt),
        in_specs=in_specs,
        out_specs=out_specs,
        out_shape=out_shape,
        compiler_params=_params(56 * 2**20, 2),
        name="proj_a",
    )(x, *consts, wa["qtab"], *tail, wa["ctab"], wa["stab"], wa["gqm"])


def _mla_attn_kernel(qt_ref, k_ref, vt_ref, o_ref, s_sc, acc_sc):
    nq = qt_ref.shape[2]
    nk = vt_ref.shape[2]
    tk = vt_ref.shape[4]
    tq = qt_ref.shape[4]

    def scores(qi, kj):
        return _dot(k_ref[0, 0, kj * tk:(kj + 1) * tk, :], qt_ref[0, 0, qi])

    s_sc[0] = scores(0, 0)

    def q_tile(qi, carry):
        m = jnp.full((1, tq), NEG, F32)
        l = jnp.zeros((1, tq), F32)
        for kj in range(nk):
            cur = kj % 2
            if kj + 1 < nk:
                s_sc[1 - cur] = scores(qi, kj + 1)
            else:
                s_sc[1 - cur] = scores(jnp.minimum(qi + 1, nq - 1), 0)
            st = s_sc[cur]
            m_new = jnp.maximum(m, jnp.max(st, axis=0, keepdims=True))
            alpha = jnp.exp2(m - m_new)
            p = jnp.exp2(st - m_new)
            l = alpha * l + jnp.sum(p, axis=0, keepdims=True)
            pv = _dot(vt_ref[0, 0, kj], p.astype(BF16))
            if kj == 0:
                acc_sc[...] = pv
            else:
                acc_sc[...] = alpha * acc_sc[...] + pv
            m = m_new
        o_ref[0, 0, qi] = (acc_sc[...] * (1.0 / l)).astype(BF16)
        return carry

    lax.fori_loop(0, nq, q_tile, 0)


def _mla_attn(qt, k, vt):
    b, nh, nt, _, ts = qt.shape
    s = k.shape[2]
    return pl.pallas_call(
        _mla_attn_kernel,
        grid=(b, nh),
        in_specs=[
            pl.BlockSpec((1, 1, nt, MLA_QK, ts), lambda i, j: (i, j, 0, 0, 0)),
            pl.BlockSpec((1, 1, s, MLA_QK), lambda i, j: (i, j, 0, 0)),
            pl.BlockSpec((1, 1, nt, MLA_V, ts), lambda i, j: (i, j, 0, 0, 0)),
        ],
        out_specs=pl.BlockSpec((1, 1, nt, MLA_V, ts), lambda i, j: (i, j, 0, 0, 0)),
        out_shape=jax.ShapeDtypeStruct((b, nh, nt, MLA_V, ts), BF16),
        scratch_shapes=[
            pltpu.VMEM((2, ts, ts), F32),
            pltpu.VMEM((MLA_V, ts), F32),
        ],
        compiler_params=_params(40 * 2**20, 2),
        name="mla_attn",
    )(qt, k, vt)


def _post_kernel(x_ref, mixt_ref, qmt_ref, sgt_ref, km_ref, vmt_ref, wot_ref, o_ref):
    ts = x_ref.shape[1]
    mixt = mixt_ref[...].reshape(MIX_WIDTH, ts)
    memo = []
    for h in range(MEM_HEADS):
        lo = h * MEM_HEAD_DIM
        st = _dot(km_ref[0, h], qmt_ref[0, lo:lo + MEM_HEAD_DIM, :])
        p = jnp.exp(st - jnp.max(st, axis=0, keepdims=True))
        l = jnp.sum(p, axis=0, keepdims=True)
        memo.append(_dot(vmt_ref[0, h], p.astype(BF16)) / l)
    sg = sgt_ref[0]
    bmix = (mixt.astype(F32) * sg[:MIX_WIDTH].astype(F32)).astype(BF16)
    bmem = (jnp.concatenate(memo, axis=0) * sg[MIX_WIDTH:].astype(F32)).astype(BF16)
    outt = _dot(wot_ref[:, :MIX_WIDTH], bmix) + _dot(wot_ref[:, MIX_WIDTH:], bmem)
    o_ref[0] = x_ref[0] + outt.T


def _post(x, mixt, mixt_spec, qmt, sgt, km, vmt, wot, ts):
    b, s, _ = x.shape
    return pl.pallas_call(
        _post_kernel,
        grid=(b, s // ts),
        in_specs=[
            pl.BlockSpec((1, ts, D_MODEL), lambda i, j: (i, j, 0)),
            mixt_spec,
            pl.BlockSpec((1, MEM_WIDTH, ts), lambda i, j: (i, 0, j)),
            pl.BlockSpec((1, BRANCH_WIDTH, ts), lambda i, j: (i, 0, j)),
            pl.BlockSpec((1, MEM_HEADS, N_MEM, MEM_HEAD_DIM), lambda i, j: (i, 0, 0, 0)),
            pl.BlockSpec((1, MEM_HEADS, MEM_HEAD_DIM, N_MEM), lambda i, j: (i, 0, 0, 0)),
            _full(wot.shape),
        ],
        out_specs=pl.BlockSpec((1, ts, D_MODEL), lambda i, j: (i, j, 0)),
        out_shape=jax.ShapeDtypeStruct(x.shape, x.dtype),
        compiler_params=_params(48 * 2**20, 2),
        name="post",
    )(x, mixt, qmt, sgt, km, vmt, wot)


def _proj_b_kernel(x_ref, gin_ref, wbt_ref, gq_ref, gk_ref, gqm_ref,
                   qt_ref, k_ref, vt_ref, qmt_ref, sgt_ref):
    h = _normed_input(x_ref, gin_ref)
    qt = _dot_nt(wbt_ref[:SWA_Q_W], h)
    for hh in range(SWA_HEADS):
        lo = hh * SWA_HEAD_DIM
        qh = qt[lo:lo + SWA_HEAD_DIM]
        qt_ref[0, lo:lo + SWA_HEAD_DIM, :] = (qh * gq_ref[...] * _rms_scale_rows(qh)).astype(BF16)
    kvt = _dot_nt(wbt_ref[SWA_Q_W:SWA_Q_W + 2 * SWA_KV_W], h)
    kparts = []
    for g in range(SWA_KV_HEADS):
        kh = kvt[g * SWA_HEAD_DIM:(g + 1) * SWA_HEAD_DIM]
        kparts.append(kh * gk_ref[...] * _rms_scale_rows(kh))
    k_ref[0] = jnp.concatenate(kparts, axis=0).T.astype(BF16)
    vt_ref[0] = kvt[SWA_KV_W:].astype(BF16)
    _mem_query_and_gate(_dot_nt(wbt_ref[SWA_Q_W + 2 * SWA_KV_W:], h), gqm_ref, qmt_ref, sgt_ref)


def _proj_b(x, wb, ts):
    b, s, _ = x.shape
    consts = [wb["gin"], wb["wbt"], wb["gq"], wb["gk"], wb["gqm"]]
    return pl.pallas_call(
        _proj_b_kernel,
        grid=(b, s // ts),
        in_specs=[pl.BlockSpec((1, ts, D_MODEL), lambda i, j: (i, j, 0))] + [_full(c.shape) for c in consts],
        out_specs=[
            pl.BlockSpec((1, SWA_Q_W, ts), lambda i, j: (i, 0, j)),
            pl.BlockSpec((1, ts, SWA_KV_W), lambda i, j: (i, j, 0)),
            pl.BlockSpec((1, SWA_KV_W, ts), lambda i, j: (i, 0, j)),
            pl.BlockSpec((1, MEM_WIDTH, ts), lambda i, j: (i, 0, j)),
            pl.BlockSpec((1, BRANCH_WIDTH, ts), lambda i, j: (i, 0, j)),
        ],
        out_shape=[
            jax.ShapeDtypeStruct((b, SWA_Q_W, s), BF16),
            jax.ShapeDtypeStruct((b, s, SWA_KV_W), BF16),
            jax.ShapeDtypeStruct((b, SWA_KV_W, s), BF16),
            jax.ShapeDtypeStruct((b, MEM_WIDTH, s), BF16),
            jax.ShapeDtypeStruct((b, BRANCH_WIDTH, s), BF16),
        ],
        compiler_params=_params(56 * 2**20, 2),
        name="proj_b",
    )(x, *consts)


def _alibi_slope(h):
    return 2.0 ** (-8.0 * (h + 1) / SWA_HEADS)


def _swa_kernel(sink_ref, qt_ref, k0_ref, k1_ref, k2_ref, k3_ref, v0_ref, v1_ref, v2_ref, v3_ref,
                t_ref, o_ref):
    i = pl.program_id(1)
    n_kblocks = pl.num_programs(1) * (SWA_Q_TILE // SWA_K_BLOCK)
    k_all = jnp.concatenate([k0_ref[0], k1_ref[0], k2_ref[0], k3_ref[0]], axis=0)
    vt_all = jnp.concatenate([v0_ref[0], v1_ref[0], v2_ref[0], v3_ref[0]], axis=1)
    nkeys = k_all.shape[0]
    row = lax.broadcasted_iota(jnp.int32, (nkeys, 1), 0)
    key_block = i * (SWA_Q_TILE // SWA_K_BLOCK) - 1 + row // SWA_K_BLOCK
    valid = (key_block >= 0) & (key_block < n_kblocks)
    t = t_ref[...]
    for g in range(SWA_KV_HEADS):
        k_g = k_all[:, g * SWA_HEAD_DIM:(g + 1) * SWA_HEAD_DIM]
        vt_g = vt_all[g * SWA_HEAD_DIM:(g + 1) * SWA_HEAD_DIM]
        for hq in range(SWA_GROUP):
            h = g * SWA_GROUP + hq
            lo = h * SWA_HEAD_DIM
            st = _dot(k_g, qt_ref[0, lo:lo + SWA_HEAD_DIM, :])
            st = jnp.where(valid, st + _alibi_slope(h) * t, NEG)
            sink = sink_ref[h]
            m = jnp.maximum(jnp.max(st, axis=0, keepdims=True), sink)
            p = jnp.exp(st - m)
            l = jnp.sum(p, axis=0, keepdims=True) + jnp.exp(sink - m)
            ot = _dot(vt_g, p.astype(BF16)) / l
            o_ref[0, lo:lo + SWA_HEAD_DIM, :] = ot.astype(BF16)


def _swa_attn(qt, k, vt, sink, ttab):
    b, _, s = qt.shape
    tq = SWA_Q_TILE
    per_tile = tq // SWA_K_BLOCK
    last = s // SWA_K_BLOCK - 1

    def kblock(o):
        return lambda i, j, *_: (i, jnp.clip(j * per_tile - 1 + o, 0, last), 0)

    def vblock(o):
        return lambda i, j, *_: (i, 0, jnp.clip(j * per_tile - 1 + o, 0, last))

    grid_spec = pltpu.PrefetchScalarGridSpec(
        num_scalar_prefetch=1,
        grid=(b, s // tq),
        in_specs=(
            [pl.BlockSpec((1, SWA_Q_W, tq), lambda i, j, *_: (i, 0, j))]
            + [pl.BlockSpec((1, SWA_K_BLOCK, SWA_KV_W), kblock(o)) for o in range(SWA_N_KBLOCKS)]
            + [pl.BlockSpec((1, SWA_KV_W, SWA_K_BLOCK), vblock(o)) for o in range(SWA_N_KBLOCKS)]
            + [pl.BlockSpec(ttab.shape, lambda i, j, *_: (0, 0))]
        ),
        out_specs=pl.BlockSpec((1, SWA_Q_W, tq), lambda i, j, *_: (i, 0, j)),
    )
    return pl.pallas_call(
        _swa_kernel,
        grid_spec=grid_spec,
        out_shape=jax.ShapeDtypeStruct((b, SWA_Q_W, s), BF16),
        compiler_params=_params(32 * 2**20, 2),
        name="swa_attn",
    )(sink, qt, k, k, k, k, vt, vt, vt, vt, ttab)


def _col(v):
    return v.astype(F32)[:, None]


def _row(v):
    return v.astype(F32)[None, :]


def _prep_layer_a(seq, norm_in, a_w_in, a_q_a_norm, a_w_q_b, a_kv_a_norm, a_w_kv_b, a_q_norm, a_k_norm, mem_q_norm):
    kr_lo = MLA_Q_RANK + MLA_KV_RANK
    qm_lo = kr_lo + MLA_ROPE
    kr_cols = a_w_in[:, kr_lo:qm_lo]
    w1 = jnp.concatenate([a_w_in[:, :kr_lo], kr_cols, kr_cols], axis=1).astype(BF16)
    wkv = a_w_kv_b.reshape(MLA_KV_RANK, MLA_HEADS, MLA_NOPE + MLA_V)
    inv = 1.0 / (ROPE_THETA ** (jnp.arange(0, MLA_ROPE, 2, dtype=F32) / MLA_ROPE))
    ang = jnp.arange(seq, dtype=F32)[:, None] * inv[None, :]
    cos, sin = jnp.cos(ang), jnp.sin(ang)
    scale = MLA_QK ** -0.5 * math.log2(math.e)
    g1 = _col(a_q_norm[MLA_NOPE:MLA_NOPE + MLA_HALF]) * scale
    g2 = _col(a_q_norm[MLA_NOPE + MLA_HALF:]) * scale
    qtab = jnp.concatenate([g1 * cos.T, g2 * sin.T, g2 * cos.T, g1 * sin.T], axis=0)
    return {
        "gin": _row(norm_in),
        "w1": w1,
        "wgt": a_w_in[:, qm_lo:].T.astype(BF16),
        "gqa": _row(a_q_a_norm),
        "wqbt": a_w_q_b.T.astype(BF16),
        "gkva": _row(a_kv_a_norm),
        "wkn": wkv[:, :, :MLA_NOPE].reshape(MLA_KV_RANK, MLA_HEADS * MLA_NOPE).astype(BF16),
        "wvt": wkv[:, :, MLA_NOPE:].reshape(MLA_KV_RANK, MLA_HEADS * MLA_V).T.astype(BF16),
        "gqn": _col(a_q_norm[:MLA_NOPE]) * scale,
        "qtab": qtab,
        "gkn": _row(a_k_norm[:MLA_NOPE]),
        "gkr": _row(jnp.tile(a_k_norm[MLA_NOPE:], 2)),
        "ctab": jnp.tile(cos, (1, 4)),
        "stab": jnp.concatenate([-sin, sin, -sin, sin], axis=1),
        "gqm": _col(mem_q_norm) * (MEM_HEAD_DIM ** -0.5),
    }


def _prep_layer_b(norm_in, b_w_in, b_q_norm, b_k_norm, mem_q_norm):
    return {
        "gin": _row(norm_in),
        "wbt": b_w_in.T.astype(BF16),
        "gq": _col(b_q_norm) * (SWA_HEAD_DIM ** -0.5),
        "gk": _col(b_k_norm),
        "gqm": _col(mem_q_norm) * (MEM_HEAD_DIM ** -0.5),
    }


def _prep_mem(mem_norm, w_mem_kv, mem_k_norm):
    return {
        "g": _row(mem_norm),
        "wk": w_mem_kv[:, :MEM_WIDTH].astype(BF16),
        "wvt": w_mem_kv[:, MEM_WIDTH:].T.astype(BF16),
        "gk": _row(mem_k_norm),
    }


def _swa_table():
    r = jnp.arange(SWA_N_KBLOCKS * SWA_K_BLOCK)[:, None]
    c = jnp.arange(SWA_Q_TILE)[None, :]
    rel = r - WINDOW - c
    return jnp.where(jnp.abs(rel) <= WINDOW, -jnp.abs(rel).astype(F32), NEG)


def _trunk(x, mem, wa, wb, wmem, wots, sink, ttab):
    ts = min(SEQ_TILE, x.shape[1])
    b = x.shape[0]
    nt = x.shape[1] // ts

    km, vmt = _memkv(mem, **wmem[0])
    qt, k, vt, qmt, sgt = _proj_a(x, wa, ts)
    mixt = _mla_attn(qt, k, vt)
    mix_spec = pl.BlockSpec((1, MLA_HEADS, 1, MLA_V, ts), lambda i, j: (i, 0, j, 0, 0))
    x = _post(x, mixt, mix_spec, qmt, sgt, km, vmt, wots[0], ts)

    km, vmt = _memkv(mem, **wmem[1])
    qt, k, vt, qmt, sgt = _proj_b(x, wb, ts)
    mixt = _swa_attn(qt, k, vt, sink, ttab)
    mix_spec = pl.BlockSpec((1, MIX_WIDTH, ts), lambda i, j: (i, 0, j))
    return _post(x, mixt, mix_spec, qmt, sgt, km, vmt, wots[1], ts)


def kernel(x_prompt, x_sample, mem_prompt, mem_sample, norm_in, w_out, mem_norm, w_mem_kv, mem_q_norm, mem_k_norm, a_w_in, a_q_a_norm, a_w_q_b, a_kv_a_norm, a_w_kv_b, a_q_norm, a_k_norm, b_w_in, b_q_norm, b_k_norm, b_sink):
    assert norm_in.shape[0] == 2 and a_w_in.shape[0] == 1 and b_w_in.shape[0] == 1
    assert x_prompt.shape[1] == x_sample.shape[1]
    seq = x_prompt.shape[1]
    assert seq % SWA_Q_TILE == 0 and seq % min(SEQ_TILE, seq) == 0
    wa = _prep_layer_a(seq, norm_in[0], a_w_in[0], a_q_a_norm[0], a_w_q_b[0], a_kv_a_norm[0], a_w_kv_b[0],
                       a_q_norm[0], a_k_norm[0], mem_q_norm[0])
    wb = _prep_layer_b(norm_in[1], b_w_in[0], b_q_norm[0], b_k_norm[0], mem_q_norm[1])
    wmem = [_prep_mem(mem_norm[i], w_mem_kv[i], mem_k_norm[i]) for i in range(2)]
    wots = [w_out[i].T.astype(BF16) for i in range(2)]
    sink = b_sink[0].astype(F32)
    ttab = _swa_table()
    y_prompt = _trunk(x_prompt, mem_prompt, wa, wb, wmem, wots, sink, ttab)
    y_sample = _trunk(x_sample, mem_sample, wa, wb, wmem, wots, sink, ttab)
    return (y_prompt, y_sample)
```

```python
import functools
import math

import jax
import jax.numpy as jnp
from jax import lax
from jax.experimental import pallas as pl
from jax.experimental.pallas import tpu as pltpu

D_MODEL = 1024
N_MEM = 256
MEM_HEADS = 4
MEM_HEAD_DIM = 128
MEM_WIDTH = MEM_HEADS * MEM_HEAD_DIM

MLA_HEADS = 8
MLA_Q_RANK = 384
MLA_KV_RANK = 256
MLA_NOPE = 128
MLA_ROPE = 64
MLA_HALF = MLA_ROPE // 2
MLA_V = 128
MLA_QK = MLA_NOPE + MLA_ROPE
ROPE_THETA = 10000.0

SWA_HEADS = 16
SWA_KV_HEADS = 2
SWA_HEAD_DIM = 64
SWA_GROUP = SWA_HEADS // SWA_KV_HEADS
SWA_Q_W = SWA_HEADS * SWA_HEAD_DIM
SWA_KV_W = SWA_KV_HEADS * SWA_HEAD_DIM
WINDOW = 128

MIX_WIDTH = 1024
BRANCH_WIDTH = MIX_WIDTH + MEM_WIDTH
EPS = 1e-6
NEG = -1e30

V7X_LANES = 128
V7X_VMEM_BYTES = 64 * 1024 * 1024

SEQ_TILE = 512
SWA_Q_TILE = 256
SWA_K_BLOCK = 128
SWA_N_KBLOCKS = (SWA_Q_TILE + 2 * WINDOW) // SWA_K_BLOCK
SWA_BAND_KEYS = SWA_K_BLOCK + 2 * WINDOW
SWA_HEADS_PER_CHAIN = 2
SWA_CHAIN_LANES = SWA_HEADS_PER_CHAIN * SWA_K_BLOCK
MLA_SCORE_SLOTS = 4
SWA_SCORE_SLOTS = 4
LOG2E = math.log2(math.e)
ONES_ROWS = 16

F32 = jnp.float32
BF16 = jnp.bfloat16

_NT = (((1,), (1,)), ((), ()))


def _vmem_limit(nbytes):
    return int(min(nbytes, V7X_VMEM_BYTES - 8 * 1024 * 1024))


def _params(nbytes, ndims):
    return pltpu.CompilerParams(
        dimension_semantics=("arbitrary",) * ndims,
        vmem_limit_bytes=_vmem_limit(nbytes))


def _dot(a, b):
    return jnp.dot(a, b, preferred_element_type=F32)


def _dot_nt(a, b):
    return lax.dot_general(a, b, _NT, preferred_element_type=F32)


def _rms_lanes(x):
    return x * lax.rsqrt(jnp.mean(x * x, axis=-1, keepdims=True) + EPS)


def _rms_scale_rows(x):
    return lax.rsqrt(jnp.mean(x * x, axis=0, keepdims=True) + EPS)


def _full(shape):
    return pl.BlockSpec(shape, lambda *_: (0,) * len(shape))


def _memkv_kernel(mem_ref, g_ref, wk_ref, wvt_ref, gk_ref, km_ref, vmt_ref):
    mn = (_rms_lanes(mem_ref[0]) * g_ref[...]).astype(BF16)
    k = _dot(mn, wk_ref[...])
    vt = _dot_nt(wvt_ref[...], mn)
    for h in range(MEM_HEADS):
        lo = h * MEM_HEAD_DIM
        kh = k[:, lo:lo + MEM_HEAD_DIM]
        km_ref[0, h] = (_rms_lanes(kh) * gk_ref[...]).astype(BF16)
        vmt_ref[0, h, :MEM_HEAD_DIM, :] = vt[lo:lo + MEM_HEAD_DIM].astype(BF16)
        vmt_ref[0, h, MEM_HEAD_DIM:, :] = jnp.ones((ONES_ROWS, N_MEM), BF16)


def _memkv(mem, g, wk, wvt, gk):
    b = mem.shape[0]
    return pl.pallas_call(
        _memkv_kernel,
        grid=(b,),
        in_specs=[
            pl.BlockSpec((1, N_MEM, D_MODEL), lambda i: (i, 0, 0)),
            _full((1, D_MODEL)),
            _full((D_MODEL, MEM_WIDTH)),
            _full((MEM_WIDTH, D_MODEL)),
            _full((1, MEM_HEAD_DIM)),
        ],
        out_specs=[
            pl.BlockSpec((1, MEM_HEADS, N_MEM, MEM_HEAD_DIM), lambda i: (i, 0, 0, 0)),
            pl.BlockSpec((1, MEM_HEADS, MEM_HEAD_DIM + ONES_ROWS, N_MEM), lambda i: (i, 0, 0, 0)),
        ],
        out_shape=[
            jax.ShapeDtypeStruct((b, MEM_HEADS, N_MEM, MEM_HEAD_DIM), BF16),
            jax.ShapeDtypeStruct((b, MEM_HEADS, MEM_HEAD_DIM + ONES_ROWS, N_MEM), BF16),
        ],
        compiler_params=_params(24 * 2**20, 1),
        name="memkv",
    )(mem, g, wk, wvt, gk)


def _normed_input(x_ref, gin_ref):
    return (_rms_lanes(x_ref[0]) * gin_ref[...]).astype(BF16)


def _mem_query_and_gate(zt, gqm_ref, qmt_ref, sgt_ref):
    for h in range(MEM_HEADS):
        lo = h * MEM_HEAD_DIM
        qm = zt[lo:lo + MEM_HEAD_DIM]
        qmt_ref[0, lo:lo + MEM_HEAD_DIM, :] = (qm * gqm_ref[...] * _rms_scale_rows(qm)).astype(BF16)
    g = zt[MEM_WIDTH:]
    sgt_ref[0] = (g * jax.nn.sigmoid(g)).astype(BF16)


def _proj_a_kernel(x_ref, gin_ref, w1_ref, wgt_ref, gqa_ref, wqbt_ref, gkva_ref, wkn_ref, wvt_ref,
                   gqn_ref, qtab_ref, gkn_ref, gkr_ref, ctab_ref, stab_ref, gqm_ref,
                   qt_ref, k_ref, vt_ref, qmt_ref, sgt_ref):
    h = _normed_input(x_ref, gin_ref)
    z1 = _dot(h, w1_ref[...])
    kv_lo = MLA_Q_RANK
    kr_lo = MLA_Q_RANK + MLA_KV_RANK
    cqn = (_rms_lanes(z1[:, :kv_lo]) * gqa_ref[...]).astype(BF16)
    ckvn = (_rms_lanes(z1[:, kv_lo:kr_lo]) * gkva_ref[...]).astype(BF16)
    kr = z1[:, kr_lo:]

    qt = _dot_nt(wqbt_ref[...], cqn)
    a1 = qtab_ref[0 * MLA_HALF:1 * MLA_HALF]
    b1 = qtab_ref[1 * MLA_HALF:2 * MLA_HALF]
    a2 = qtab_ref[2 * MLA_HALF:3 * MLA_HALF]
    b2 = qtab_ref[3 * MLA_HALF:4 * MLA_HALF]
    for hh in range(MLA_HEADS):
        qh = qt[hh * MLA_QK:(hh + 1) * MLA_QK]
        r = _rms_scale_rows(qh)
        x1 = qh[MLA_NOPE:MLA_NOPE + MLA_HALF]
        x2 = qh[MLA_NOPE + MLA_HALF:]
        qt_ref[0, hh, 0, :MLA_NOPE, :] = (qh[:MLA_NOPE] * gqn_ref[...] * r).astype(BF16)
        qt_ref[0, hh, 0, MLA_NOPE:MLA_NOPE + MLA_HALF, :] = ((x1 * a1 - x2 * b1) * r).astype(BF16)
        qt_ref[0, hh, 0, MLA_NOPE + MLA_HALF:, :] = ((x2 * a2 + x1 * b2) * r).astype(BF16)

    kn = _dot(ckvn, wkn_ref[...])
    vt = _dot_nt(wvt_ref[...], ckvn)
    krg = kr * gkr_ref[...]
    kro = krg * ctab_ref[...] + pltpu.roll(krg, MLA_HALF, 1) * stab_ref[...]
    ss_r = 0.5 * jnp.sum(kr * kr, axis=1, keepdims=True)
    for hh in range(MLA_HEADS):
        knh = kn[:, hh * MLA_NOPE:(hh + 1) * MLA_NOPE]
        ss = jnp.sum(knh * knh, axis=1, keepdims=True) + ss_r
        r = lax.rsqrt(ss * (1.0 / MLA_QK) + EPS)
        k_ref[0, hh, :, :MLA_NOPE] = (knh * gkn_ref[...] * r).astype(BF16)
        k_ref[0, hh, :, MLA_NOPE:] = (kro[:, :MLA_ROPE] * r).astype(BF16)
        vt_ref[0, hh, 0, :MLA_V, :] = vt[hh * MLA_V:(hh + 1) * MLA_V].astype(BF16)
        vt_ref[0, hh, 0, MLA_V:, :] = jnp.ones((ONES_ROWS, vt.shape[1]), BF16)

    _mem_query_and_gate(_dot_nt(wgt_ref[...], h), gqm_ref, qmt_ref, sgt_ref)


def _proj_a(x, wa, ts):
    b, s, _ = x.shape
    nt = s // ts
    consts = [wa["gin"], wa["w1"], wa["wgt"], wa["gqa"], wa["wqbt"], wa["gkva"], wa["wkn"], wa["wvt"], wa["gqn"]]
    tail = [wa["gkn"], wa["gkr"]]
    in_specs = (
        [pl.BlockSpec((1, ts, D_MODEL), lambda i, j: (i, j, 0))]
        + [_full(c.shape) for c in consts]
        + [pl.BlockSpec((4 * MLA_HALF, ts), lambda i, j: (0, j))]
        + [_full(c.shape) for c in tail]
        + [pl.BlockSpec((ts, V7X_LANES), lambda i, j: (j, 0))] * 2
        + [_full(wa["gqm"].shape)]
    )
    out_shape = [
        jax.ShapeDtypeStruct((b, MLA_HEADS, nt, MLA_QK, ts), BF16),
        jax.ShapeDtypeStruct((b, MLA_HEADS, s, MLA_QK), BF16),
        jax.ShapeDtypeStruct((b, MLA_HEADS, nt, MLA_V + ONES_ROWS, ts), BF16),
        jax.ShapeDtypeStruct((b, MEM_WIDTH, s), BF16),
        jax.ShapeDtypeStruct((b, BRANCH_WIDTH, s), BF16),
    ]
    out_specs = [
        pl.BlockSpec((1, MLA_HEADS, 1, MLA_QK, ts), lambda i, j: (i, 0, j, 0, 0)),
        pl.BlockSpec((1, MLA_HEADS, ts, MLA_QK), lambda i, j: (i, 0, j, 0)),
        pl.BlockSpec((1, MLA_HEADS, 1, MLA_V + ONES_ROWS, ts), lambda i, j: (i, 0, j, 0, 0)),
        pl.BlockSpec((1, MEM_WIDTH, ts), lambda i, j: (i, 0, j)),
        pl.BlockSpec((1, BRANCH_WIDTH, ts), lambda i, j: (i, 0, j)),
    ]
    return pl.pallas_call(
        _proj_a_kernel,
        grid=(b, nt),
        in_specs=in_specs,
        out_specs=out_specs,
        out_shape=out_shape,
        compiler_params=_params(56 * 2**20, 2),
        name="proj_a",
    )(x, *consts, wa["qtab"], *tail, wa["ctab"], wa["stab"], wa["gqm"])


def _mla_attn_kernel(qt_ref, k_ref, vt_ref, o_ref, s_sc, mx_sc, acc_sc):
    nq = qt_ref.shape[2]
    nk = vt_ref.shape[2]
    tk = vt_ref.shape[4]
    tq = qt_ref.shape[4]
    nslots = s_sc.shape[0]
    ahead = nslots - 1

    def put_scores(qi, kj):
        st = _dot(k_ref[0, 0, kj * tk:(kj + 1) * tk, :], qt_ref[0, 0, qi])
        s_sc[kj % nslots] = st
        mx_sc[kj % nslots] = jnp.max(st, axis=0, keepdims=True)

    for kj in range(ahead):
        put_scores(0, kj)

    def q_tile(qi, carry):
        m = jnp.full((1, tq), NEG, F32)
        for kj in range(nk):
            cur = kj % nslots
            if kj + ahead < nk:
                put_scores(qi, kj + ahead)
            else:
                put_scores(jnp.minimum(qi + 1, nq - 1), kj + ahead - nk)
            m_new = jnp.maximum(m, mx_sc[cur])
            p = jnp.exp2((s_sc[cur] - m_new).astype(BF16))
            pv = _dot(vt_ref[0, 0, kj], p)
            if kj == 0:
                acc_sc[...] = pv
            else:
                acc_sc[...] = jnp.exp2(m - m_new) * acc_sc[...] + pv
            m = m_new
        acc = acc_sc[...]
        o_ref[0, 0, qi] = (acc[:MLA_V] * (1.0 / acc[MLA_V:MLA_V + 1])).astype(BF16)
        return carry

    lax.fori_loop(0, nq, q_tile, 0)


def _mla_attn(qt, k, vt):
    b, nh, nt, _, ts = qt.shape
    s = k.shape[2]
    nslots = min(MLA_SCORE_SLOTS, nt)
    assert nt % nslots == 0
    return pl.pallas_call(
        _mla_attn_kernel,
        grid=(b, nh),
        in_specs=[
            pl.BlockSpec((1, 1, nt, MLA_QK, ts), lambda i, j: (i, j, 0, 0, 0)),
            pl.BlockSpec((1, 1, s, MLA_QK), lambda i, j: (i, j, 0, 0)),
            pl.BlockSpec((1, 1, nt, MLA_V + ONES_ROWS, ts), lambda i, j: (i, j, 0, 0, 0)),
        ],
        out_specs=pl.BlockSpec((1, 1, nt, MLA_V, ts), lambda i, j: (i, j, 0, 0, 0)),
        out_shape=jax.ShapeDtypeStruct((b, nh, nt, MLA_V, ts), BF16),
        scratch_shapes=[
            pltpu.VMEM((nslots, ts, ts), F32),
            pltpu.VMEM((nslots, 1, ts), F32),
            pltpu.VMEM((MLA_V + ONES_ROWS, ts), F32),
        ],
        compiler_params=_params(40 * 2**20, 2),
        name="mla_attn",
    )(qt, k, vt)


def _post_kernel(x_ref, mixt_ref, qmt_ref, sgt_ref, km_ref, vmt_ref, wot_ref, o_ref, s_sc):
    ts = x_ref.shape[1]
    for h in range(MEM_HEADS):
        lo = h * MEM_HEAD_DIM
        s_sc[h] = _dot(km_ref[0, h], qmt_ref[0, lo:lo + MEM_HEAD_DIM, :])
    sg = sgt_ref[0]
    bmix = mixt_ref[...].reshape(MIX_WIDTH, ts) * sg[:MIX_WIDTH]
    memo = []
    for h in range(MEM_HEADS):
        st = s_sc[h]
        p = jnp.exp2((st - jnp.max(st, axis=0, keepdims=True)).astype(BF16))
        pv = _dot(vmt_ref[0, h], p)
        memo.append(pv[:MEM_HEAD_DIM] * (1.0 / pv[MEM_HEAD_DIM:MEM_HEAD_DIM + 1]))
    bmem = (jnp.concatenate(memo, axis=0) * sg[MIX_WIDTH:].astype(F32)).astype(BF16)
    outt = _dot(wot_ref[:, :MIX_WIDTH], bmix) + _dot(wot_ref[:, MIX_WIDTH:], bmem)
    o_ref[0] = x_ref[0] + outt.T


def _post(x, mixt, mixt_spec, qmt, sgt, km, vmt, wot, ts):
    b, s, _ = x.shape
    return pl.pallas_call(
        _post_kernel,
        grid=(b, s // ts),
        in_specs=[
            pl.BlockSpec((1, ts, D_MODEL), lambda i, j: (i, j, 0)),
            mixt_spec,
            pl.BlockSpec((1, MEM_WIDTH, ts), lambda i, j: (i, 0, j)),
            pl.BlockSpec((1, BRANCH_WIDTH, ts), lambda i, j: (i, 0, j)),
            pl.BlockSpec((1, MEM_HEADS, N_MEM, MEM_HEAD_DIM), lambda i, j: (i, 0, 0, 0)),
            pl.BlockSpec((1, MEM_HEADS, MEM_HEAD_DIM + ONES_ROWS, N_MEM), lambda i, j: (i, 0, 0, 0)),
            _full(wot.shape),
        ],
        out_specs=pl.BlockSpec((1, ts, D_MODEL), lambda i, j: (i, j, 0)),
        out_shape=jax.ShapeDtypeStruct(x.shape, x.dtype),
        scratch_shapes=[pltpu.VMEM((MEM_HEADS, N_MEM, ts), F32)],
        compiler_params=_params(48 * 2**20, 2),
        name="post",
    )(x, mixt, qmt, sgt, km, vmt, wot)


def _proj_b_kernel(x_ref, gin_ref, wbt_ref, gq_ref, gk_ref, gqm_ref,
                   qt_ref, k_ref, vt_ref, qmt_ref, sgt_ref):
    h = _normed_input(x_ref, gin_ref)
    qt = _dot_nt(wbt_ref[:SWA_Q_W], h)
    for hh in range(SWA_HEADS):
        lo = hh * SWA_HEAD_DIM
        qh = qt[lo:lo + SWA_HEAD_DIM]
        qn = (qh * gq_ref[...] * _rms_scale_rows(qh)).astype(BF16)
        g, hq = divmod(hh, SWA_GROUP)
        for c in range(qn.shape[1] // SWA_K_BLOCK):
            qt_ref[0, g, c, :, hq * SWA_K_BLOCK:(hq + 1) * SWA_K_BLOCK] = qn[:, c * SWA_K_BLOCK:(c + 1) * SWA_K_BLOCK]
    kvt = _dot_nt(wbt_ref[SWA_Q_W:SWA_Q_W + 2 * SWA_KV_W], h)
    kparts = []
    for g in range(SWA_KV_HEADS):
        kh = kvt[g * SWA_HEAD_DIM:(g + 1) * SWA_HEAD_DIM]
        kparts.append(kh * gk_ref[...] * _rms_scale_rows(kh))
    k_ref[0] = jnp.concatenate(kparts, axis=0).T.astype(BF16)
    vt_ref[0] = kvt[SWA_KV_W:].astype(BF16)
    _mem_query_and_gate(_dot_nt(wbt_ref[SWA_Q_W + 2 * SWA_KV_W:], h), gqm_ref, qmt_ref, sgt_ref)


def _proj_b(x, wb, ts):
    b, s, _ = x.shape
    consts = [wb["gin"], wb["wbt"], wb["gq"], wb["gk"], wb["gqm"]]
    return pl.pallas_call(
        _proj_b_kernel,
        grid=(b, s // ts),
        in_specs=[pl.BlockSpec((1, ts, D_MODEL), lambda i, j: (i, j, 0))] + [_full(c.shape) for c in consts],
        out_specs=[
            pl.BlockSpec((1, SWA_KV_HEADS, ts // SWA_K_BLOCK, SWA_HEAD_DIM, SWA_GROUP * SWA_K_BLOCK),
                         lambda i, j: (i, 0, j, 0, 0)),
            pl.BlockSpec((1, ts, SWA_KV_W), lambda i, j: (i, j, 0)),
            pl.BlockSpec((1, SWA_KV_W, ts), lambda i, j: (i, 0, j)),
            pl.BlockSpec((1, MEM_WIDTH, ts), lambda i, j: (i, 0, j)),
            pl.BlockSpec((1, BRANCH_WIDTH, ts), lambda i, j: (i, 0, j)),
        ],
        out_shape=[
            jax.ShapeDtypeStruct((b, SWA_KV_HEADS, s // SWA_K_BLOCK, SWA_HEAD_DIM, SWA_GROUP * SWA_K_BLOCK), BF16),
            jax.ShapeDtypeStruct((b, s, SWA_KV_W), BF16),
            jax.ShapeDtypeStruct((b, SWA_KV_W, s), BF16),
            jax.ShapeDtypeStruct((b, MEM_WIDTH, s), BF16),
            jax.ShapeDtypeStruct((b, BRANCH_WIDTH, s), BF16),
        ],
        compiler_params=_params(56 * 2**20, 2),
        name="proj_b",
    )(x, *consts)


def _alibi_slope(h):
    return 2.0 ** (-8.0 * (h + 1) / SWA_HEADS)


def _swa_kernel(qt_ref, k0_ref, k1_ref, k2_ref, k3_ref, v0_ref, v1_ref, v2_ref, v3_ref,
                bias_lo_ref, bias_hi_ref, sink_ref, o_ref, s_sc, mx_sc):
    k_all = jnp.concatenate([k0_ref[0], k1_ref[0], k2_ref[0], k3_ref[0]], axis=0)
    vt_all = jnp.concatenate([v0_ref[0], v1_ref[0], v2_ref[0], v3_ref[0]], axis=1)
    bias_refs = (bias_lo_ref, bias_hi_ref)
    chains = [(sb, g, c) for sb in range(len(bias_refs)) for g in range(SWA_KV_HEADS)
              for c in range(SWA_GROUP // SWA_HEADS_PER_CHAIN)]
    k_g, vt_g = {}, {}
    for sb in range(len(bias_refs)):
        k0 = sb * SWA_K_BLOCK
        for g in range(SWA_KV_HEADS):
            k_g[sb, g] = k_all[k0:k0 + SWA_BAND_KEYS, g * SWA_HEAD_DIM:(g + 1) * SWA_HEAD_DIM]
            vt_g[sb, g] = jnp.concatenate(
                [vt_all[g * SWA_HEAD_DIM:(g + 1) * SWA_HEAD_DIM, k0:k0 + SWA_BAND_KEYS],
                 jnp.ones((ONES_ROWS, SWA_BAND_KEYS), BF16)], axis=0)

    def put_scores(slot, chain):
        sb, g, c = chain
        lanes = slice(c * SWA_CHAIN_LANES, (c + 1) * SWA_CHAIN_LANES)
        st = _dot(k_g[sb, g], qt_ref[0, g, sb, :, lanes]) + bias_refs[sb][0, g, :, lanes]
        s_sc[slot] = st
        mx_sc[slot] = jnp.max(st, axis=0, keepdims=True)

    nslots = s_sc.shape[0]
    ahead = nslots - 1
    for t in range(min(ahead, len(chains))):
        put_scores(t, chains[t])
    for t, (sb, g, c) in enumerate(chains):
        if t + ahead < len(chains):
            put_scores((t + ahead) % nslots, chains[t + ahead])
        sink = sink_ref[g, :, c * SWA_CHAIN_LANES:(c + 1) * SWA_CHAIN_LANES]
        m = jnp.maximum(mx_sc[t % nslots], sink)
        p = jnp.exp2((s_sc[t % nslots] - m).astype(BF16))
        pv = _dot(vt_g[sb, g], p)
        l = pv[SWA_HEAD_DIM:SWA_HEAD_DIM + 1] + jnp.exp2(sink - m)
        ot = pv[:SWA_HEAD_DIM] * (1.0 / l)
        for hh in range(SWA_HEADS_PER_CHAIN):
            lo = ((g * SWA_GROUP) + c * SWA_HEADS_PER_CHAIN + hh) * SWA_HEAD_DIM
            o_ref[0, lo:lo + SWA_HEAD_DIM, sb * SWA_K_BLOCK:(sb + 1) * SWA_K_BLOCK] = (
                ot[:, hh * SWA_K_BLOCK:(hh + 1) * SWA_K_BLOCK].astype(BF16))


def _swa_attn(qt, k, vt, bias, sink):
    b, s = k.shape[0], k.shape[1]
    tq = SWA_Q_TILE
    per_tile = tq // SWA_K_BLOCK
    last = s // SWA_K_BLOCK - 1
    last_tile = s // tq - 1

    def kblock(o):
        return lambda i, j: (i, jnp.clip(j * per_tile - 1 + o, 0, last), 0)

    def vblock(o):
        return lambda i, j: (i, 0, jnp.clip(j * per_tile - 1 + o, 0, last))

    bias_block = (1,) + bias.shape[1:]
    return pl.pallas_call(
        _swa_kernel,
        grid=(b, s // tq),
        in_specs=(
            [pl.BlockSpec((1, SWA_KV_HEADS, per_tile, SWA_HEAD_DIM, SWA_GROUP * SWA_K_BLOCK),
                          lambda i, j: (i, 0, j, 0, 0))]
            + [pl.BlockSpec((1, SWA_K_BLOCK, SWA_KV_W), kblock(o)) for o in range(SWA_N_KBLOCKS)]
            + [pl.BlockSpec((1, SWA_KV_W, SWA_K_BLOCK), vblock(o)) for o in range(SWA_N_KBLOCKS)]
            + [pl.BlockSpec(bias_block, lambda i, j: (jnp.where(j == 0, 0, 1), 0, 0, 0)),
               pl.BlockSpec(bias_block, lambda i, j: (jnp.where(j == last_tile, 2, 1), 0, 0, 0)),
               _full(sink.shape)]
        ),
        out_specs=pl.BlockSpec((1, SWA_Q_W, tq), lambda i, j: (i, 0, j)),
        out_shape=jax.ShapeDtypeStruct((b, SWA_Q_W, s), BF16),
        scratch_shapes=[pltpu.VMEM((SWA_SCORE_SLOTS, SWA_BAND_KEYS, SWA_CHAIN_LANES), F32),
                        pltpu.VMEM((SWA_SCORE_SLOTS, 1, SWA_CHAIN_LANES), F32)],
        compiler_params=_params(40 * 2**20, 2),
        name="swa_attn",
    )(qt, k, k, k, k, vt, vt, vt, vt, bias, bias, sink)


def _col(v):
    return v.astype(F32)[:, None]


def _row(v):
    return v.astype(F32)[None, :]


def _prep_layer_a(seq, norm_in, a_w_in, a_q_a_norm, a_w_q_b, a_kv_a_norm, a_w_kv_b, a_q_norm, a_k_norm, mem_q_norm):
    kr_lo = MLA_Q_RANK + MLA_KV_RANK
    qm_lo = kr_lo + MLA_ROPE
    kr_cols = a_w_in[:, kr_lo:qm_lo]
    w1 = jnp.concatenate([a_w_in[:, :kr_lo], kr_cols, kr_cols], axis=1).astype(BF16)
    wkv = a_w_kv_b.reshape(MLA_KV_RANK, MLA_HEADS, MLA_NOPE + MLA_V)
    inv = 1.0 / (ROPE_THETA ** (jnp.arange(0, MLA_ROPE, 2, dtype=F32) / MLA_ROPE))
    ang = jnp.arange(seq, dtype=F32)[:, None] * inv[None, :]
    cos, sin = jnp.cos(ang), jnp.sin(ang)
    scale = MLA_QK ** -0.5 * math.log2(math.e)
    g1 = _col(a_q_norm[MLA_NOPE:MLA_NOPE + MLA_HALF]) * scale
    g2 = _col(a_q_norm[MLA_NOPE + MLA_HALF:]) * scale
    qtab = jnp.concatenate([g1 * cos.T, g2 * sin.T, g2 * cos.T, g1 * sin.T], axis=0)
    return {
        "gin": _row(norm_in),
        "w1": w1,
        "wgt": a_w_in[:, qm_lo:].T.astype(BF16),
        "gqa": _row(a_q_a_norm),
        "wqbt": a_w_q_b.T.astype(BF16),
        "gkva": _row(a_kv_a_norm),
        "wkn": wkv[:, :, :MLA_NOPE].reshape(MLA_KV_RANK, MLA_HEADS * MLA_NOPE).astype(BF16),
        "wvt": wkv[:, :, MLA_NOPE:].reshape(MLA_KV_RANK, MLA_HEADS * MLA_V).T.astype(BF16),
        "gqn": _col(a_q_norm[:MLA_NOPE]) * scale,
        "qtab": qtab,
        "gkn": _row(a_k_norm[:MLA_NOPE]),
        "gkr": _row(jnp.tile(a_k_norm[MLA_NOPE:], 2)),
        "ctab": jnp.tile(cos, (1, 4)),
        "stab": jnp.concatenate([-sin, sin, -sin, sin], axis=1),
        "gqm": _col(mem_q_norm) * (MEM_HEAD_DIM ** -0.5 * LOG2E),
    }


def _prep_layer_b(norm_in, b_w_in, b_q_norm, b_k_norm, mem_q_norm):
    return {
        "gin": _row(norm_in),
        "wbt": b_w_in.T.astype(BF16),
        "gq": _col(b_q_norm) * (SWA_HEAD_DIM ** -0.5 * LOG2E),
        "gk": _col(b_k_norm),
        "gqm": _col(mem_q_norm) * (MEM_HEAD_DIM ** -0.5 * LOG2E),
    }


def _prep_mem(mem_norm, w_mem_kv, mem_k_norm):
    return {
        "g": _row(mem_norm),
        "wk": w_mem_kv[:, :MEM_WIDTH].astype(BF16),
        "wvt": w_mem_kv[:, MEM_WIDTH:].T.astype(BF16),
        "gk": _row(mem_k_norm),
    }


def _swa_bias():
    r = jnp.arange(SWA_BAND_KEYS)[:, None]
    c = jnp.arange(SWA_K_BLOCK)[None, :]
    rel = r - WINDOW - c
    dist = jnp.abs(rel).astype(F32)
    slopes = jnp.array([_alibi_slope(h) for h in range(SWA_HEADS)], F32).reshape(SWA_KV_HEADS, SWA_GROUP)
    alibi = -(LOG2E * slopes)[:, None, :, None] * dist[None, :, None, :]
    in_window = (jnp.abs(rel) <= WINDOW)[None, None, :, None, :]
    key_block = (r // SWA_K_BLOCK)[None, None, :, None, :]
    variant = jnp.arange(3)[:, None, None, None, None]
    in_seq = ~(((variant == 0) & (key_block == 0)) | ((variant == 2) & (key_block == 2)))
    bias = jnp.where(in_window & in_seq, alibi[None], NEG)
    return bias.reshape(3, SWA_KV_HEADS, SWA_BAND_KEYS, SWA_GROUP * SWA_K_BLOCK)


def _swa_sink_lanes(b_sink):
    s = (b_sink.astype(F32) * LOG2E).reshape(SWA_KV_HEADS, 1, SWA_GROUP, 1)
    return jnp.broadcast_to(s, (SWA_KV_HEADS, 1, SWA_GROUP, SWA_K_BLOCK)).reshape(SWA_KV_HEADS, 1, -1)


def _trunk(x, mem, wa, wb, wmem, wots, sink, bias):
    ts = min(SEQ_TILE, x.shape[1])
    b = x.shape[0]
    nt = x.shape[1] // ts

    km, vmt = _memkv(mem, **wmem[0])
    qt, k, vt, qmt, sgt = _proj_a(x, wa, ts)
    mixt = _mla_attn(qt, k, vt)
    mix_spec = pl.BlockSpec((1, MLA_HEADS, 1, MLA_V, ts), lambda i, j: (i, 0, j, 0, 0))
    x = _post(x, mixt, mix_spec, qmt, sgt, km, vmt, wots[0], ts)

    km, vmt = _memkv(mem, **wmem[1])
    qt, k, vt, qmt, sgt = _proj_b(x, wb, ts)
    mixt = _swa_attn(qt, k, vt, bias, sink)
    mix_spec = pl.BlockSpec((1, MIX_WIDTH, ts), lambda i, j: (i, 0, j))
    return _post(x, mixt, mix_spec, qmt, sgt, km, vmt, wots[1], ts)


def kernel(x_prompt, x_sample, mem_prompt, mem_sample, norm_in, w_out, mem_norm, w_mem_kv, mem_q_norm, mem_k_norm, a_w_in, a_q_a_norm, a_w_q_b, a_kv_a_norm, a_w_kv_b, a_q_norm, a_k_norm, b_w_in, b_q_norm, b_k_norm, b_sink):
    assert norm_in.shape[0] == 2 and a_w_in.shape[0] == 1 and b_w_in.shape[0] == 1
    assert x_prompt.shape[1] == x_sample.shape[1]
    seq = x_prompt.shape[1]
    assert seq % SWA_Q_TILE == 0 and seq % min(SEQ_TILE, seq) == 0
    wa = _prep_layer_a(seq, norm_in[0], a_w_in[0], a_q_a_norm[0], a_w_q_b[0], a_kv_a_norm[0], a_w_kv_b[0],
                       a_q_norm[0], a_k_norm[0], mem_q_norm[0])
    wb = _prep_layer_b(norm_in[1], b_w_in[0], b_q_norm[0], b_k_norm[0], mem_q_norm[1])
    wmem = [_prep_mem(mem_norm[i], w_mem_kv[i], mem_k_norm[i]) for i in range(2)]
    wots = [w_out[i].T.astype(BF16) for i in range(2)]
    sink = _swa_sink_lanes(b_sink[0])
    bias = _swa_bias()
    y_prompt = _trunk(x_prompt, mem_prompt, wa, wb, wmem, wots, sink, bias)
    y_sample = _trunk(x_sample, mem_sample, wa, wb, wmem, wots, sink, bias)
    return (y_prompt, y_sample)
```

```python
import functools
import math

import jax
import jax.numpy as jnp
from jax import lax
from jax.experimental import pallas as pl
from jax.experimental.pallas import tpu as pltpu

D_MODEL = 1024
N_MEM = 256
MEM_HEADS = 4
MEM_HEAD_DIM = 128
MEM_WIDTH = MEM_HEADS * MEM_HEAD_DIM

MLA_HEADS = 8
MLA_Q_RANK = 384
MLA_KV_RANK = 256
MLA_NOPE = 128
MLA_ROPE = 64
MLA_HALF = MLA_ROPE // 2
MLA_V = 128
MLA_QK = MLA_NOPE + MLA_ROPE
ROPE_THETA = 10000.0

SWA_HEADS = 16
SWA_KV_HEADS = 2
SWA_HEAD_DIM = 64
SWA_GROUP = SWA_HEADS // SWA_KV_HEADS
SWA_Q_W = SWA_HEADS * SWA_HEAD_DIM
SWA_KV_W = SWA_KV_HEADS * SWA_HEAD_DIM
WINDOW = 128

MIX_WIDTH = 1024
BRANCH_WIDTH = MIX_WIDTH + MEM_WIDTH
EPS = 1e-6
NEG = -1e30

V7X_LANES = 128
V7X_VMEM_BYTES = 64 * 1024 * 1024

SEQ_TILE = 512
PROJ_A_SUB_TILE = 512
PROJ_B_SUB_TILE = 256
SWA_Q_TILE = 512
SWA_K_BLOCK = 128
SWA_WINDOW_BLOCKS = 2 * WINDOW // SWA_K_BLOCK
SWA_BAND_KEYS = SWA_K_BLOCK + 2 * WINDOW
SWA_HEADS_PER_CHAIN = 2
SWA_CHAIN_LANES = SWA_HEADS_PER_CHAIN * SWA_K_BLOCK
MLA_SCORE_SLOTS = 4
SWA_SCORE_SLOTS = 4
LOG2E = math.log2(math.e)
MLA_Q_SCALE = MLA_QK ** -0.5 * LOG2E
SWA_Q_SCALE = SWA_HEAD_DIM ** -0.5 * LOG2E
MEM_Q_SCALE = MEM_HEAD_DIM ** -0.5 * LOG2E
ONES_ROWS = 16

F32 = jnp.float32
BF16 = jnp.bfloat16

_NT = (((1,), (1,)), ((), ()))


def _vmem_limit(nbytes):
    return int(min(nbytes, V7X_VMEM_BYTES - 8 * 1024 * 1024))


def _params(nbytes, ndims):
    return pltpu.CompilerParams(
        dimension_semantics=("arbitrary",) * ndims,
        vmem_limit_bytes=_vmem_limit(nbytes))


def _dot(a, b):
    return jnp.dot(a, b, preferred_element_type=F32)


def _dot_nt(a, b):
    return lax.dot_general(a, b, _NT, preferred_element_type=F32)


def _rms_lanes(x):
    return x * lax.rsqrt(jnp.mean(x * x, axis=-1, keepdims=True) + EPS)


def _rms_scale_rows(x):
    return lax.rsqrt(jnp.mean(x * x, axis=0, keepdims=True) + EPS)


def _full(shape):
    return pl.BlockSpec(shape, lambda *_: (0,) * len(shape))


def _memkv_kernel(mem_ref, wk_ref, wvt_ref, gk_ref, km_ref, vmt_ref):
    mn = _rms_lanes(mem_ref[0]).astype(BF16)
    k = _dot(mn, wk_ref[...])
    vt = _dot_nt(wvt_ref[...], mn)
    for h in range(MEM_HEADS):
        lo = h * MEM_HEAD_DIM
        kh = k[:, lo:lo + MEM_HEAD_DIM]
        km_ref[0, h] = (_rms_lanes(kh) * gk_ref[...]).astype(BF16)
        vmt_ref[0, h, :MEM_HEAD_DIM, :] = vt[lo:lo + MEM_HEAD_DIM].astype(BF16)
        vmt_ref[0, h, MEM_HEAD_DIM:, :] = jnp.ones((ONES_ROWS, N_MEM), BF16)


def _memkv(mem, wk, wvt, gk):
    b = mem.shape[0]
    return pl.pallas_call(
        _memkv_kernel,
        grid=(b,),
        in_specs=[
            pl.BlockSpec((1, N_MEM, D_MODEL), lambda i: (i, 0, 0)),
            _full((D_MODEL, MEM_WIDTH)),
            _full((MEM_WIDTH, D_MODEL)),
            _full((1, MEM_HEAD_DIM)),
        ],
        out_specs=[
            pl.BlockSpec((1, MEM_HEADS, N_MEM, MEM_HEAD_DIM), lambda i: (i, 0, 0, 0)),
            pl.BlockSpec((1, MEM_HEADS, MEM_HEAD_DIM + ONES_ROWS, N_MEM), lambda i: (i, 0, 0, 0)),
        ],
        out_shape=[
            jax.ShapeDtypeStruct((b, MEM_HEADS, N_MEM, MEM_HEAD_DIM), BF16),
            jax.ShapeDtypeStruct((b, MEM_HEADS, MEM_HEAD_DIM + ONES_ROWS, N_MEM), BF16),
        ],
        compiler_params=_params(24 * 2**20, 1),
        name="memkv",
    )(mem, wk, wvt, gk)


def _normed_input(x_ref, tok):
    return _rms_lanes(x_ref[0, tok, :]).astype(BF16)


def _mem_query_and_gate(zt, qmt_ref, sgt_ref, tok):
    for h in range(MEM_HEADS):
        lo = h * MEM_HEAD_DIM
        qm = zt[lo:lo + MEM_HEAD_DIM]
        qmt_ref[0, lo:lo + MEM_HEAD_DIM, tok] = (qm * (_rms_scale_rows(qm) * MEM_Q_SCALE)).astype(BF16)
    hg = 0.5 * zt[MEM_WIDTH:]
    sgt_ref[0, :, tok] = (hg + hg * jnp.tanh(hg)).astype(BF16)


def _sub_tiles(ts, sub):
    sub = min(sub, ts)
    return [slice(c * sub, (c + 1) * sub) for c in range(ts // sub)]


def _proj_a_kernel(x_ref, w1_ref, wgt_ref, wqbt_ref, wkn_ref, wvt_ref,
                   qtab_ref, gkn_ref, gkr_ref, ctab_ref, stab_ref,
                   qt_ref, k_ref, vt_ref, qmt_ref, sgt_ref):
    kv_lo = MLA_Q_RANK
    kr_lo = MLA_Q_RANK + MLA_KV_RANK
    for tok in _sub_tiles(x_ref.shape[1], PROJ_A_SUB_TILE):
        h = _normed_input(x_ref, tok)
        _mem_query_and_gate(_dot_nt(wgt_ref[...], h), qmt_ref, sgt_ref, tok)
        z1 = _dot(h, w1_ref[...])
        cqn = _rms_lanes(z1[:, :kv_lo]).astype(BF16)
        ckvn = _rms_lanes(z1[:, kv_lo:kr_lo]).astype(BF16)
        kr = z1[:, kr_lo:]

        qt = _dot_nt(wqbt_ref[...], cqn)
        a1 = qtab_ref[0 * MLA_HALF:1 * MLA_HALF, tok]
        b1 = qtab_ref[1 * MLA_HALF:2 * MLA_HALF, tok]
        a2 = qtab_ref[2 * MLA_HALF:3 * MLA_HALF, tok]
        b2 = qtab_ref[3 * MLA_HALF:4 * MLA_HALF, tok]
        for hh in range(MLA_HEADS):
            qh = qt[hh * MLA_QK:(hh + 1) * MLA_QK]
            r = _rms_scale_rows(qh)
            x1 = qh[MLA_NOPE:MLA_NOPE + MLA_HALF]
            x2 = qh[MLA_NOPE + MLA_HALF:]
            qt_ref[0, hh, 0, :MLA_NOPE, tok] = (qh[:MLA_NOPE] * (r * MLA_Q_SCALE)).astype(BF16)
            qt_ref[0, hh, 0, MLA_NOPE:MLA_NOPE + MLA_HALF, tok] = ((x1 * a1 - x2 * b1) * r).astype(BF16)
            qt_ref[0, hh, 0, MLA_NOPE + MLA_HALF:, tok] = ((x2 * a2 + x1 * b2) * r).astype(BF16)

        kn = _dot(ckvn, wkn_ref[...])
        vt = _dot_nt(wvt_ref[...], ckvn)
        krg = kr * gkr_ref[...]
        kro = krg * ctab_ref[tok, :] + pltpu.roll(krg, MLA_HALF, 1) * stab_ref[tok, :]
        ss_r = 0.5 * jnp.sum(kr * kr, axis=1, keepdims=True)
        for hh in range(MLA_HEADS):
            knh = kn[:, hh * MLA_NOPE:(hh + 1) * MLA_NOPE]
            ss = jnp.sum(knh * knh, axis=1, keepdims=True) + ss_r
            r = lax.rsqrt(ss * (1.0 / MLA_QK) + EPS)
            k_ref[0, hh, tok, :MLA_NOPE] = (knh * gkn_ref[...] * r).astype(BF16)
            k_ref[0, hh, tok, MLA_NOPE:] = (kro[:, :MLA_ROPE] * r).astype(BF16)
            vt_ref[0, hh, 0, :MLA_V, tok] = vt[hh * MLA_V:(hh + 1) * MLA_V].astype(BF16)
            vt_ref[0, hh, 0, MLA_V:, tok] = jnp.ones((ONES_ROWS, vt.shape[1]), BF16)


def _proj_a(x, wa, ts):
    b, s, _ = x.shape
    nt = s // ts
    consts = [wa["w1"], wa["wgt"], wa["wqbt"], wa["wkn"], wa["wvt"]]
    tail = [wa["gkn"], wa["gkr"]]
    in_specs = (
        [pl.BlockSpec((1, ts, D_MODEL), lambda i, j: (i, j, 0))]
        + [_full(c.shape) for c in consts]
        + [pl.BlockSpec((4 * MLA_HALF, ts), lambda i, j: (0, j))]
        + [_full(c.shape) for c in tail]
        + [pl.BlockSpec((ts, V7X_LANES), lambda i, j: (j, 0))] * 2
    )
    out_shape = [
        jax.ShapeDtypeStruct((b, MLA_HEADS, nt, MLA_QK, ts), BF16),
        jax.ShapeDtypeStruct((b, MLA_HEADS, s, MLA_QK), BF16),
        jax.ShapeDtypeStruct((b, MLA_HEADS, nt, MLA_V + ONES_ROWS, ts), BF16),
        jax.ShapeDtypeStruct((b, MEM_WIDTH, s), BF16),
        jax.ShapeDtypeStruct((b, BRANCH_WIDTH, s), BF16),
    ]
    out_specs = [
        pl.BlockSpec((1, MLA_HEADS, 1, MLA_QK, ts), lambda i, j: (i, 0, j, 0, 0)),
        pl.BlockSpec((1, MLA_HEADS, ts, MLA_QK), lambda i, j: (i, 0, j, 0)),
        pl.BlockSpec((1, MLA_HEADS, 1, MLA_V + ONES_ROWS, ts), lambda i, j: (i, 0, j, 0, 0)),
        pl.BlockSpec((1, MEM_WIDTH, ts), lambda i, j: (i, 0, j)),
        pl.BlockSpec((1, BRANCH_WIDTH, ts), lambda i, j: (i, 0, j)),
    ]
    return pl.pallas_call(
        _proj_a_kernel,
        grid=(b, nt),
        in_specs=in_specs,
        out_specs=out_specs,
        out_shape=out_shape,
        compiler_params=_params(56 * 2**20, 2),
        name="proj_a",
    )(x, *consts, wa["qtab"], *tail, wa["ctab"], wa["stab"])


def _mla_attn_kernel(qt_ref, k_ref, vt_ref, o_ref, s_sc, mx_sc, acc_sc):
    nq = qt_ref.shape[2]
    nk = vt_ref.shape[2]
    tk = vt_ref.shape[4]
    tq = qt_ref.shape[4]
    nslots = s_sc.shape[0]
    ahead = nslots - 1

    def put_scores(qi, kj):
        st = _dot(k_ref[0, 0, kj * tk:(kj + 1) * tk, :], qt_ref[0, 0, qi])
        s_sc[kj % nslots] = st
        mx_sc[kj % nslots] = jnp.max(st, axis=0, keepdims=True)

    for kj in range(ahead):
        put_scores(0, kj)

    def q_tile(qi, is_last):
        m = jnp.full((1, tq), NEG, F32)
        for kj in range(nk):
            cur = kj % nslots
            if kj + ahead < nk:
                put_scores(qi, kj + ahead)
            elif not is_last:
                put_scores(qi + 1, kj + ahead - nk)
            m_new = jnp.maximum(m, mx_sc[cur])
            p = jnp.exp2((s_sc[cur] - m_new).astype(BF16))
            pv = _dot(vt_ref[0, 0, kj], p)
            if kj == 0:
                acc_sc[...] = pv
            else:
                acc_sc[...] = jnp.exp2(m - m_new) * acc_sc[...] + pv
            m = m_new
        acc = acc_sc[...]
        o_ref[0, 0, qi] = (acc[:MLA_V] * (1.0 / acc[MLA_V:MLA_V + 1])).astype(BF16)

    def q_tile_with_lookahead(qi, carry):
        q_tile(qi, is_last=False)
        return carry

    lax.fori_loop(0, nq - 1, q_tile_with_lookahead, 0)
    q_tile(nq - 1, is_last=True)


def _mla_attn(qt, k, vt):
    b, nh, nt, _, ts = qt.shape
    s = k.shape[2]
    nslots = min(MLA_SCORE_SLOTS, nt)
    assert nt % nslots == 0
    return pl.pallas_call(
        _mla_attn_kernel,
        grid=(b, nh),
        in_specs=[
            pl.BlockSpec((1, 1, nt, MLA_QK, ts), lambda i, j: (i, j, 0, 0, 0)),
            pl.BlockSpec((1, 1, s, MLA_QK), lambda i, j: (i, j, 0, 0)),
            pl.BlockSpec((1, 1, nt, MLA_V + ONES_ROWS, ts), lambda i, j: (i, j, 0, 0, 0)),
        ],
        out_specs=pl.BlockSpec((1, 1, nt, MLA_V, ts), lambda i, j: (i, j, 0, 0, 0)),
        out_shape=jax.ShapeDtypeStruct((b, nh, nt, MLA_V, ts), BF16),
        scratch_shapes=[
            pltpu.VMEM((nslots, ts, ts), F32),
            pltpu.VMEM((nslots, 1, ts), F32),
            pltpu.VMEM((MLA_V + ONES_ROWS, ts), F32),
        ],
        compiler_params=_params(40 * 2**20, 2),
        name="mla_attn",
    )(qt, k, vt)


def _post_kernel(x_ref, mixt_ref, qmt_ref, sgt_ref, km_ref, vmt_ref, wot_ref, o_ref, s_sc):
    ts = x_ref.shape[1]
    for h in range(MEM_HEADS):
        lo = h * MEM_HEAD_DIM
        s_sc[h] = _dot(km_ref[0, h], qmt_ref[0, lo:lo + MEM_HEAD_DIM, :])
    sg = sgt_ref[0]
    bmix = mixt_ref[...].reshape(MIX_WIDTH, ts) * sg[:MIX_WIDTH]
    memo = []
    for h in range(MEM_HEADS):
        st = s_sc[h]
        p = jnp.exp2((st - jnp.max(st, axis=0, keepdims=True)).astype(BF16))
        pv = _dot(vmt_ref[0, h], p)
        memo.append(pv[:MEM_HEAD_DIM] * (1.0 / pv[MEM_HEAD_DIM:MEM_HEAD_DIM + 1]))
    bmem = (jnp.concatenate(memo, axis=0) * sg[MIX_WIDTH:].astype(F32)).astype(BF16)
    outt = _dot(wot_ref[:, :MIX_WIDTH], bmix) + _dot(wot_ref[:, MIX_WIDTH:], bmem)
    o_ref[0] = x_ref[0] + outt.T


def _post(x, mixt, mixt_spec, qmt, sgt, km, vmt, wot, ts):
    b, s, _ = x.shape
    return pl.pallas_call(
        _post_kernel,
        grid=(b, s // ts),
        in_specs=[
            pl.BlockSpec((1, ts, D_MODEL), lambda i, j: (i, j, 0)),
            mixt_spec,
            pl.BlockSpec((1, MEM_WIDTH, ts), lambda i, j: (i, 0, j)),
            pl.BlockSpec((1, BRANCH_WIDTH, ts), lambda i, j: (i, 0, j)),
            pl.BlockSpec((1, MEM_HEADS, N_MEM, MEM_HEAD_DIM), lambda i, j: (i, 0, 0, 0)),
            pl.BlockSpec((1, MEM_HEADS, MEM_HEAD_DIM + ONES_ROWS, N_MEM), lambda i, j: (i, 0, 0, 0)),
            _full(wot.shape),
        ],
        out_specs=pl.BlockSpec((1, ts, D_MODEL), lambda i, j: (i, j, 0)),
        out_shape=jax.ShapeDtypeStruct(x.shape, x.dtype),
        scratch_shapes=[pltpu.VMEM((MEM_HEADS, N_MEM, ts), F32)],
        compiler_params=_params(48 * 2**20, 2),
        name="post",
    )(x, mixt, qmt, sgt, km, vmt, wot)


def _proj_b_kernel(x_ref, wbt_ref, gk_ref, qt_ref, k_ref, vt_ref, qmt_ref, sgt_ref):
    for tok in _sub_tiles(x_ref.shape[1], PROJ_B_SUB_TILE):
        h = _normed_input(x_ref, tok)
        _mem_query_and_gate(_dot_nt(wbt_ref[SWA_Q_W + 2 * SWA_KV_W:], h), qmt_ref, sgt_ref, tok)
        qt = _dot_nt(wbt_ref[:SWA_Q_W], h)
        for hh in range(SWA_HEADS):
            lo = hh * SWA_HEAD_DIM
            qh = qt[lo:lo + SWA_HEAD_DIM]
            qn = (qh * (_rms_scale_rows(qh) * SWA_Q_SCALE)).astype(BF16)
            g, hq = divmod(hh, SWA_GROUP)
            for c in range(qn.shape[1] // SWA_K_BLOCK):
                qt_ref[0, g, tok.start // SWA_K_BLOCK + c, :, hq * SWA_K_BLOCK:(hq + 1) * SWA_K_BLOCK] = (
                    qn[:, c * SWA_K_BLOCK:(c + 1) * SWA_K_BLOCK])
        kvt = _dot_nt(wbt_ref[SWA_Q_W:SWA_Q_W + 2 * SWA_KV_W], h)
        kparts = []
        for g in range(SWA_KV_HEADS):
            kh = kvt[g * SWA_HEAD_DIM:(g + 1) * SWA_HEAD_DIM]
            kparts.append(kh * _rms_scale_rows(kh))
        k_ref[0, tok, :] = (jnp.concatenate(kparts, axis=0).T * gk_ref[...]).astype(BF16)
        vt_ref[0, :, tok] = kvt[SWA_KV_W:].astype(BF16)


def _proj_b(x, wb, ts):
    b, s, _ = x.shape
    consts = [wb["wbt"], wb["gk"]]
    return pl.pallas_call(
        _proj_b_kernel,
        grid=(b, s // ts),
        in_specs=[pl.BlockSpec((1, ts, D_MODEL), lambda i, j: (i, j, 0))] + [_full(c.shape) for c in consts],
        out_specs=[
            pl.BlockSpec((1, SWA_KV_HEADS, ts // SWA_K_BLOCK, SWA_HEAD_DIM, SWA_GROUP * SWA_K_BLOCK),
                         lambda i, j: (i, 0, j, 0, 0)),
            pl.BlockSpec((1, ts, SWA_KV_W), lambda i, j: (i, j, 0)),
            pl.BlockSpec((1, SWA_KV_W, ts), lambda i, j: (i, 0, j)),
            pl.BlockSpec((1, MEM_WIDTH, ts), lambda i, j: (i, 0, j)),
            pl.BlockSpec((1, BRANCH_WIDTH, ts), lambda i, j: (i, 0, j)),
        ],
        out_shape=[
            jax.ShapeDtypeStruct((b, SWA_KV_HEADS, s // SWA_K_BLOCK, SWA_HEAD_DIM, SWA_GROUP * SWA_K_BLOCK), BF16),
            jax.ShapeDtypeStruct((b, s, SWA_KV_W), BF16),
            jax.ShapeDtypeStruct((b, SWA_KV_W, s), BF16),
            jax.ShapeDtypeStruct((b, MEM_WIDTH, s), BF16),
            jax.ShapeDtypeStruct((b, BRANCH_WIDTH, s), BF16),
        ],
        compiler_params=_params(56 * 2**20, 2),
        name="proj_b",
    )(x, *consts)


def _alibi_slope(h):
    return 2.0 ** (-8.0 * (h + 1) / SWA_HEADS)


def _swa_kernel(qt_ref, *refs):
    per_tile = qt_ref.shape[2]
    n_kblocks = per_tile + SWA_WINDOW_BLOCKS
    k_refs, refs = refs[:n_kblocks], refs[n_kblocks:]
    v_refs, refs = refs[:n_kblocks], refs[n_kblocks:]
    bias_lo_ref, bias_mid_ref, bias_hi_ref, sink_ref, o_ref, s_sc, mx_sc = refs
    k_all = jnp.concatenate([r[0] for r in k_refs], axis=0)
    vt_all = jnp.concatenate([r[0] for r in v_refs], axis=1)
    bias_refs = (bias_lo_ref,) + (bias_mid_ref,) * (per_tile - 2) + (bias_hi_ref,)
    chains = [(sb, g, c) for sb in range(per_tile) for g in range(SWA_KV_HEADS)
              for c in range(SWA_GROUP // SWA_HEADS_PER_CHAIN)]
    k_g, vt_g = {}, {}
    for sb in range(len(bias_refs)):
        k0 = sb * SWA_K_BLOCK
        for g in range(SWA_KV_HEADS):
            k_g[sb, g] = k_all[k0:k0 + SWA_BAND_KEYS, g * SWA_HEAD_DIM:(g + 1) * SWA_HEAD_DIM]
            vt_g[sb, g] = jnp.concatenate(
                [vt_all[g * SWA_HEAD_DIM:(g + 1) * SWA_HEAD_DIM, k0:k0 + SWA_BAND_KEYS],
                 jnp.ones((ONES_ROWS, SWA_BAND_KEYS), BF16)], axis=0)

    def put_scores(slot, chain):
        sb, g, c = chain
        lanes = slice(c * SWA_CHAIN_LANES, (c + 1) * SWA_CHAIN_LANES)
        st = _dot(k_g[sb, g], qt_ref[0, g, sb, :, lanes]) + bias_refs[sb][0, g, :, lanes]
        s_sc[slot] = st
        mx_sc[slot] = jnp.max(st, axis=0, keepdims=True)

    nslots = s_sc.shape[0]
    ahead = nslots - 1
    for t in range(min(ahead, len(chains))):
        put_scores(t, chains[t])
    for t, (sb, g, c) in enumerate(chains):
        if t + ahead < len(chains):
            put_scores((t + ahead) % nslots, chains[t + ahead])
        sink = sink_ref[g, :, c * SWA_CHAIN_LANES:(c + 1) * SWA_CHAIN_LANES]
        m = jnp.maximum(mx_sc[t % nslots], sink)
        p = jnp.exp2((s_sc[t % nslots] - m).astype(BF16))
        pv = _dot(vt_g[sb, g], p)
        l = pv[SWA_HEAD_DIM:SWA_HEAD_DIM + 1] + jnp.exp2(sink - m)
        ot = pv[:SWA_HEAD_DIM] * (1.0 / l)
        for hh in range(SWA_HEADS_PER_CHAIN):
            lo = ((g * SWA_GROUP) + c * SWA_HEADS_PER_CHAIN + hh) * SWA_HEAD_DIM
            o_ref[0, lo:lo + SWA_HEAD_DIM, sb * SWA_K_BLOCK:(sb + 1) * SWA_K_BLOCK] = (
                ot[:, hh * SWA_K_BLOCK:(hh + 1) * SWA_K_BLOCK].astype(BF16))


def _swa_attn(qt, k, vt, bias, sink):
    b, s = k.shape[0], k.shape[1]
    tq = min(SWA_Q_TILE, s)
    per_tile = tq // SWA_K_BLOCK
    assert per_tile >= 2
    n_kblocks = per_tile + SWA_WINDOW_BLOCKS
    last = s // SWA_K_BLOCK - 1
    last_tile = s // tq - 1

    def kblock(o):
        return lambda i, j: (i, jnp.clip(j * per_tile - 1 + o, 0, last), 0)

    def vblock(o):
        return lambda i, j: (i, 0, jnp.clip(j * per_tile - 1 + o, 0, last))

    bias_block = (1,) + bias.shape[1:]
    return pl.pallas_call(
        _swa_kernel,
        grid=(b, s // tq),
        in_specs=(
            [pl.BlockSpec((1, SWA_KV_HEADS, per_tile, SWA_HEAD_DIM, SWA_GROUP * SWA_K_BLOCK),
                          lambda i, j: (i, 0, j, 0, 0))]
            + [pl.BlockSpec((1, SWA_K_BLOCK, SWA_KV_W), kblock(o)) for o in range(n_kblocks)]
            + [pl.BlockSpec((1, SWA_KV_W, SWA_K_BLOCK), vblock(o)) for o in range(n_kblocks)]
            + [pl.BlockSpec(bias_block, lambda i, j: (jnp.where(j == 0, 0, 1), 0, 0, 0)),
               pl.BlockSpec(bias_block, lambda i, j: (1, 0, 0, 0)),
               pl.BlockSpec(bias_block, lambda i, j: (jnp.where(j == last_tile, 2, 1), 0, 0, 0)),
               _full(sink.shape)]
        ),
        out_specs=pl.BlockSpec((1, SWA_Q_W, tq), lambda i, j: (i, 0, j)),
        out_shape=jax.ShapeDtypeStruct((b, SWA_Q_W, s), BF16),
        scratch_shapes=[pltpu.VMEM((SWA_SCORE_SLOTS, SWA_BAND_KEYS, SWA_CHAIN_LANES), F32),
                        pltpu.VMEM((SWA_SCORE_SLOTS, 1, SWA_CHAIN_LANES), F32)],
        compiler_params=_params(48 * 2**20, 2),
        name="swa_attn",
    )(qt, *([k] * n_kblocks), *([vt] * n_kblocks), bias, bias, bias, sink)


def _col(v):
    return v.astype(F32)[:, None]


def _row(v):
    return v.astype(F32)[None, :]


def _prep_layer_a(seq, norm_in, a_w_in, a_q_a_norm, a_w_q_b, a_kv_a_norm, a_w_kv_b, a_q_norm, a_k_norm):
    kr_lo = MLA_Q_RANK + MLA_KV_RANK
    qm_lo = kr_lo + MLA_ROPE
    w_in = _col(norm_in) * a_w_in
    kr_cols = w_in[:, kr_lo:qm_lo]
    wkv = (_col(a_kv_a_norm) * a_w_kv_b).reshape(MLA_KV_RANK, MLA_HEADS, MLA_NOPE + MLA_V)
    inv = 1.0 / (ROPE_THETA ** (jnp.arange(0, MLA_ROPE, 2, dtype=F32) / MLA_ROPE))
    ang = jnp.arange(seq, dtype=F32)[:, None] * inv[None, :]
    cos, sin = jnp.cos(ang), jnp.sin(ang)
    g1 = _col(a_q_norm[MLA_NOPE:MLA_NOPE + MLA_HALF]) * MLA_Q_SCALE
    g2 = _col(a_q_norm[MLA_NOPE + MLA_HALF:]) * MLA_Q_SCALE
    return {
        "w1": jnp.concatenate([w_in[:, :kr_lo], kr_cols, kr_cols], axis=1).astype(BF16),
        "wgt": w_in[:, qm_lo:].T.astype(BF16),
        "wqbt": (_col(a_q_a_norm) * a_w_q_b).T.astype(BF16),
        "wkn": wkv[:, :, :MLA_NOPE].reshape(MLA_KV_RANK, MLA_HEADS * MLA_NOPE).astype(BF16),
        "wvt": wkv[:, :, MLA_NOPE:].reshape(MLA_KV_RANK, MLA_HEADS * MLA_V).T.astype(BF16),
        "qtab": jnp.concatenate([g1 * cos.T, g2 * sin.T, g2 * cos.T, g1 * sin.T], axis=0),
        "gkn": _row(a_k_norm[:MLA_NOPE] * a_q_norm[:MLA_NOPE]),
        "gkr": _row(jnp.tile(a_k_norm[MLA_NOPE:], 2)),
        "ctab": jnp.tile(cos, (1, 4)),
        "stab": jnp.concatenate([-sin, sin, -sin, sin], axis=1),
    }


def _prep_layer_b(norm_in, b_w_in, b_q_norm, b_k_norm):
    return {
        "wbt": (_col(norm_in) * b_w_in).T.astype(BF16),
        "gk": _row(jnp.tile(b_k_norm * b_q_norm, SWA_KV_HEADS)),
    }


def _prep_mem(mem_norm, w_mem_kv, mem_k_norm, mem_q_norm):
    w = _col(mem_norm) * w_mem_kv
    return {
        "wk": w[:, :MEM_WIDTH].astype(BF16),
        "wvt": w[:, MEM_WIDTH:].T.astype(BF16),
        "gk": _row(mem_k_norm * mem_q_norm),
    }


def _swa_bias():
    r = jnp.arange(SWA_BAND_KEYS)[:, None]
    c = jnp.arange(SWA_K_BLOCK)[None, :]
    rel = r - WINDOW - c
    dist = jnp.abs(rel).astype(F32)
    slopes = jnp.array([_alibi_slope(h) for h in range(SWA_HEADS)], F32).reshape(SWA_KV_HEADS, SWA_GROUP)
    alibi = -(LOG2E * slopes)[:, None, :, None] * dist[None, :, None, :]
    in_window = (jnp.abs(rel) <= WINDOW)[None, None, :, None, :]
    key_block = (r // SWA_K_BLOCK)[None, None, :, None, :]
    variant = jnp.arange(3)[:, None, None, None, None]
    in_seq = ~(((variant == 0) & (key_block == 0)) | ((variant == 2) & (key_block == 2)))
    bias = jnp.where(in_window & in_seq, alibi[None], NEG)
    return bias.reshape(3, SWA_KV_HEADS, SWA_BAND_KEYS, SWA_GROUP * SWA_K_BLOCK)


def _swa_sink_lanes(b_sink):
    s = (b_sink.astype(F32) * LOG2E).reshape(SWA_KV_HEADS, 1, SWA_GROUP, 1)
    return jnp.broadcast_to(s, (SWA_KV_HEADS, 1, SWA_GROUP, SWA_K_BLOCK)).reshape(SWA_KV_HEADS, 1, -1)


def _trunk(x, mem, wa, wb, wmem, wots, sink, bias):
    ts = min(SEQ_TILE, x.shape[1])
    b = x.shape[0]
    nt = x.shape[1] // ts

    km, vmt = _memkv(mem, **wmem[0])
    qt, k, vt, qmt, sgt = _proj_a(x, wa, ts)
    mixt = _mla_attn(qt, k, vt)
    mix_spec = pl.BlockSpec((1, MLA_HEADS, 1, MLA_V, ts), lambda i, j: (i, 0, j, 0, 0))
    x = _post(x, mixt, mix_spec, qmt, sgt, km, vmt, wots[0], ts)

    km, vmt = _memkv(mem, **wmem[1])
    qt, k, vt, qmt, sgt = _proj_b(x, wb, ts)
    mixt = _swa_attn(qt, k, vt, bias, sink)
    mix_spec = pl.BlockSpec((1, MIX_WIDTH, ts), lambda i, j: (i, 0, j))
    return _post(x, mixt, mix_spec, qmt, sgt, km, vmt, wots[1], ts)


def kernel(x_prompt, x_sample, mem_prompt, mem_sample, norm_in, w_out, mem_norm, w_mem_kv, mem_q_norm, mem_k_norm, a_w_in, a_q_a_norm, a_w_q_b, a_kv_a_norm, a_w_kv_b, a_q_norm, a_k_norm, b_w_in, b_q_norm, b_k_norm, b_sink):
    assert norm_in.shape[0] == 2 and a_w_in.shape[0] == 1 and b_w_in.shape[0] == 1
    assert x_prompt.shape[1] == x_sample.shape[1]
    seq = x_prompt.shape[1]
    assert seq % min(SWA_Q_TILE, seq) == 0 and seq % min(SEQ_TILE, seq) == 0
    wa = _prep_layer_a(seq, norm_in[0], a_w_in[0], a_q_a_norm[0], a_w_q_b[0], a_kv_a_norm[0], a_w_kv_b[0],
                       a_q_norm[0], a_k_norm[0])
    wb = _prep_layer_b(norm_in[1], b_w_in[0], b_q_norm[0], b_k_norm[0])
    wmem = [_prep_mem(mem_norm[i], w_mem_kv[i], mem_k_norm[i], mem_q_norm[i]) for i in range(2)]
    wots = [w_out[i].T.astype(BF16) for i in range(2)]
    sink = _swa_sink_lanes(b_sink[0])
    bias = _swa_bias()
    y_prompt = _trunk(x_prompt, mem_prompt, wa, wb, wmem, wots, sink, bias)
    y_sample = _trunk(x_sample, mem_sample, wa, wb, wmem, wots, sink, bias)
    return (y_prompt, y_sample)
```

```python
import functools
import math

import jax
import jax.numpy as jnp
from jax import lax
from jax.experimental import pallas as pl
from jax.experimental.pallas import tpu as pltpu

D_MODEL = 1024
N_MEM = 256
MEM_HEADS = 4
MEM_HEAD_DIM = 128
MEM_WIDTH = MEM_HEADS * MEM_HEAD_DIM

MLA_HEADS = 8
MLA_Q_RANK = 384
MLA_KV_RANK = 256
MLA_NOPE = 128
MLA_ROPE = 64
MLA_HALF = MLA_ROPE // 2
MLA_V = 128
MLA_QK = MLA_NOPE + MLA_ROPE
ROPE_THETA = 10000.0

SWA_HEADS = 16
SWA_KV_HEADS = 2
SWA_HEAD_DIM = 64
SWA_GROUP = SWA_HEADS // SWA_KV_HEADS
SWA_Q_W = SWA_HEADS * SWA_HEAD_DIM
SWA_KV_W = SWA_KV_HEADS * SWA_HEAD_DIM
WINDOW = 128

MIX_WIDTH = 1024
BRANCH_WIDTH = MIX_WIDTH + MEM_WIDTH
EPS = 1e-6
NEG = -1e30

V7X_LANES = 128
V7X_VMEM_BYTES = 64 * 1024 * 1024

SEQ_TILE = 512
PROJ_A_SUB_TILE = 512
PROJ_B_SUB_TILE = 256
SWA_Q_TILE = 512
SWA_K_BLOCK = 128
SWA_WINDOW_BLOCKS = 2 * WINDOW // SWA_K_BLOCK
SWA_BAND_KEYS = SWA_K_BLOCK + 2 * WINDOW
SWA_HEADS_PER_CHAIN = 2
SWA_CHAIN_LANES = SWA_HEADS_PER_CHAIN * SWA_K_BLOCK
MLA_SCORE_SLOTS = 8
MLA_HEADS_PER_STEP = 2
SWA_SCORE_SLOTS = 4
LOG2E = math.log2(math.e)
MLA_Q_SCALE = MLA_QK ** -0.5 * LOG2E
SWA_Q_SCALE = SWA_HEAD_DIM ** -0.5 * LOG2E
MEM_Q_SCALE = MEM_HEAD_DIM ** -0.5 * LOG2E
ONES_ROWS = 16

F32 = jnp.float32
BF16 = jnp.bfloat16

_NT = (((1,), (1,)), ((), ()))
_TN = (((0,), (0,)), ((), ()))


def _vmem_limit(nbytes):
    return int(min(nbytes, V7X_VMEM_BYTES - 8 * 1024 * 1024))


def _params(nbytes, ndims):
    return pltpu.CompilerParams(
        dimension_semantics=("arbitrary",) * ndims,
        vmem_limit_bytes=_vmem_limit(nbytes))


def _dot(a, b):
    return jnp.dot(a, b, preferred_element_type=F32)


def _dot_nt(a, b):
    return lax.dot_general(a, b, _NT, preferred_element_type=F32)


def _dot_tn(a, b):
    return lax.dot_general(a, b, _TN, preferred_element_type=F32)


def _rms_lanes(x):
    return x * lax.rsqrt(jnp.mean(x * x, axis=-1, keepdims=True) + EPS)


def _rms_scale_rows(x):
    return lax.rsqrt(jnp.mean(x * x, axis=0, keepdims=True) + EPS)


def _full(shape):
    return pl.BlockSpec(shape, lambda *_: (0,) * len(shape))


def _memkv_kernel(mem_ref, wk_ref, wvt_ref, gk_ref, km_ref, vmt_ref):
    mn = _rms_lanes(mem_ref[0]).astype(BF16)
    k = _dot(mn, wk_ref[...])
    vt = _dot_nt(wvt_ref[...], mn)
    for h in range(MEM_HEADS):
        lo = h * MEM_HEAD_DIM
        kh = k[:, lo:lo + MEM_HEAD_DIM]
        km_ref[0, h] = (_rms_lanes(kh) * gk_ref[...]).astype(BF16)
        vmt_ref[0, h, :MEM_HEAD_DIM, :] = vt[lo:lo + MEM_HEAD_DIM].astype(BF16)
        vmt_ref[0, h, MEM_HEAD_DIM:, :] = jnp.ones((ONES_ROWS, N_MEM), BF16)


def _memkv(mem, wk, wvt, gk):
    b = mem.shape[0]
    return pl.pallas_call(
        _memkv_kernel,
        grid=(b,),
        in_specs=[
            pl.BlockSpec((1, N_MEM, D_MODEL), lambda i: (i, 0, 0)),
            _full((D_MODEL, MEM_WIDTH)),
            _full((MEM_WIDTH, D_MODEL)),
            _full((1, MEM_HEAD_DIM)),
        ],
        out_specs=[
            pl.BlockSpec((1, MEM_HEADS, N_MEM, MEM_HEAD_DIM), lambda i: (i, 0, 0, 0)),
            pl.BlockSpec((1, MEM_HEADS, MEM_HEAD_DIM + ONES_ROWS, N_MEM), lambda i: (i, 0, 0, 0)),
        ],
        out_shape=[
            jax.ShapeDtypeStruct((b, MEM_HEADS, N_MEM, MEM_HEAD_DIM), BF16),
            jax.ShapeDtypeStruct((b, MEM_HEADS, MEM_HEAD_DIM + ONES_ROWS, N_MEM), BF16),
        ],
        compiler_params=_params(24 * 2**20, 1),
        name="memkv",
    )(mem, wk, wvt, gk)


def _normed_input(x_ref, tok):
    return _rms_lanes(x_ref[0, tok, :]).astype(BF16)


def _mem_query_and_gate(zt, qmt_ref, sgt_ref, tok):
    for h in range(MEM_HEADS):
        lo = h * MEM_HEAD_DIM
        qm = zt[lo:lo + MEM_HEAD_DIM]
        qmt_ref[0, lo:lo + MEM_HEAD_DIM, tok] = (qm * (_rms_scale_rows(qm) * MEM_Q_SCALE)).astype(BF16)
    hg = 0.5 * zt[MEM_WIDTH:]
    sgt_ref[0, :, tok] = (hg + hg * jnp.tanh(hg)).astype(BF16)


def _sub_tiles(ts, sub):
    sub = min(sub, ts)
    return [slice(c * sub, (c + 1) * sub) for c in range(ts // sub)]


def _proj_a_kernel(x_ref, w1_ref, wgt_ref, wqbt_ref, wkn_ref, wvt_ref,
                   qtab_ref, gkn_ref, gkr_ref, ctab_ref, stab_ref,
                   qt_ref, k_ref, vt_ref, qmt_ref, sgt_ref):
    kv_lo = MLA_Q_RANK
    kr_lo = MLA_Q_RANK + MLA_KV_RANK
    for tok in _sub_tiles(x_ref.shape[1], PROJ_A_SUB_TILE):
        h = _normed_input(x_ref, tok)
        z1 = _dot(h, w1_ref[...])
        _mem_query_and_gate(_dot_nt(wgt_ref[...], h), qmt_ref, sgt_ref, tok)
        cqn = _rms_lanes(z1[:, :kv_lo]).astype(BF16)
        ckvn = _rms_lanes(z1[:, kv_lo:kr_lo]).astype(BF16)
        kr = z1[:, kr_lo:]

        qt = _dot_nt(wqbt_ref[...], cqn)
        a1 = qtab_ref[0 * MLA_HALF:1 * MLA_HALF, tok]
        b1 = qtab_ref[1 * MLA_HALF:2 * MLA_HALF, tok]
        a2 = qtab_ref[2 * MLA_HALF:3 * MLA_HALF, tok]
        b2 = qtab_ref[3 * MLA_HALF:4 * MLA_HALF, tok]
        for hh in range(MLA_HEADS):
            qh = qt[hh * MLA_QK:(hh + 1) * MLA_QK]
            r = _rms_scale_rows(qh)
            x1 = qh[MLA_NOPE:MLA_NOPE + MLA_HALF]
            x2 = qh[MLA_NOPE + MLA_HALF:]
            qt_ref[0, hh, 0, :MLA_NOPE, tok] = (qh[:MLA_NOPE] * (r * MLA_Q_SCALE)).astype(BF16)
            qt_ref[0, hh, 0, MLA_NOPE:MLA_NOPE + MLA_HALF, tok] = ((x1 * a1 - x2 * b1) * r).astype(BF16)
            qt_ref[0, hh, 0, MLA_NOPE + MLA_HALF:, tok] = ((x2 * a2 + x1 * b2) * r).astype(BF16)

        kn = _dot(ckvn, wkn_ref[...])
        vt = _dot_nt(wvt_ref[...], ckvn)
        krg = kr * gkr_ref[...]
        kro = krg * ctab_ref[tok, :] + pltpu.roll(krg, MLA_HALF, 1) * stab_ref[tok, :]
        ss_r = 0.5 * jnp.sum(kr * kr, axis=1, keepdims=True)
        for hh in range(MLA_HEADS):
            knh = kn[:, hh * MLA_NOPE:(hh + 1) * MLA_NOPE]
            ss = jnp.sum(knh * knh, axis=1, keepdims=True) + ss_r
            r = lax.rsqrt(ss * (1.0 / MLA_QK) + EPS)
            k_ref[0, hh, tok, :MLA_NOPE] = (knh * gkn_ref[...] * r).astype(BF16)
            k_ref[0, hh, tok, MLA_NOPE:] = (kro[:, :MLA_ROPE] * r).astype(BF16)
            vt_ref[0, hh, 0, :MLA_V, tok] = vt[hh * MLA_V:(hh + 1) * MLA_V].astype(BF16)
            vt_ref[0, hh, 0, MLA_V:, tok] = jnp.ones((ONES_ROWS, vt.shape[1]), BF16)


def _proj_a(x, wa, ts):
    b, s, _ = x.shape
    nt = s // ts
    consts = [wa["w1"], wa["wgt"], wa["wqbt"], wa["wkn"], wa["wvt"]]
    tail = [wa["gkn"], wa["gkr"]]
    in_specs = (
        [pl.BlockSpec((1, ts, D_MODEL), lambda i, j: (i, j, 0))]
        + [_full(c.shape) for c in consts]
        + [pl.BlockSpec((4 * MLA_HALF, ts), lambda i, j: (0, j))]
        + [_full(c.shape) for c in tail]
        + [pl.BlockSpec((ts, V7X_LANES), lambda i, j: (j, 0))] * 2
    )
    out_shape = [
        jax.ShapeDtypeStruct((b, MLA_HEADS, nt, MLA_QK, ts), BF16),
        jax.ShapeDtypeStruct((b, MLA_HEADS, s, MLA_QK), BF16),
        jax.ShapeDtypeStruct((b, MLA_HEADS, nt, MLA_V + ONES_ROWS, ts), BF16),
        jax.ShapeDtypeStruct((b, MEM_WIDTH, s), BF16),
        jax.ShapeDtypeStruct((b, BRANCH_WIDTH, s), BF16),
    ]
    out_specs = [
        pl.BlockSpec((1, MLA_HEADS, 1, MLA_QK, ts), lambda i, j: (i, 0, j, 0, 0)),
        pl.BlockSpec((1, MLA_HEADS, ts, MLA_QK), lambda i, j: (i, 0, j, 0)),
        pl.BlockSpec((1, MLA_HEADS, 1, MLA_V + ONES_ROWS, ts), lambda i, j: (i, 0, j, 0, 0)),
        pl.BlockSpec((1, MEM_WIDTH, ts), lambda i, j: (i, 0, j)),
        pl.BlockSpec((1, BRANCH_WIDTH, ts), lambda i, j: (i, 0, j)),
    ]
    return pl.pallas_call(
        _proj_a_kernel,
        grid=(b, nt),
        in_specs=in_specs,
        out_specs=out_specs,
        out_shape=out_shape,
        compiler_params=_params(56 * 2**20, 2),
        name="proj_a",
    )(x, *consts, wa["qtab"], *tail, wa["ctab"], wa["stab"])


def _mla_attn_kernel(qt_ref, k_ref, vt_ref, o_ref, s_sc, mx_sc, acc_sc):
    nq = qt_ref.shape[2]
    nk = vt_ref.shape[2]
    tk = vt_ref.shape[4]
    tq = qt_ref.shape[4]
    nslots = s_sc.shape[0]
    ahead = nslots - 1

    n_items = qt_ref.shape[1] * nq

    def put_scores(item, kj):
        hh, qi = item // nq, item % nq
        st = _dot(k_ref[0, hh, kj * tk:(kj + 1) * tk, :], qt_ref[0, hh, qi])
        s_sc[kj % nslots] = st
        mx_sc[kj % nslots] = jnp.max(st, axis=0, keepdims=True)

    for kj in range(ahead):
        put_scores(0, kj)

    def q_tile(item, is_last):
        hh, qi = item // nq, item % nq
        m = jnp.full((1, tq), NEG, F32)
        for kj in range(nk):
            cur = kj % nslots
            if kj + ahead < nk:
                put_scores(item, kj + ahead)
            elif not is_last:
                put_scores(item + 1, kj + ahead - nk)
            m_new = jnp.maximum(m, mx_sc[cur])
            p = jnp.exp2((s_sc[cur] - m_new).astype(BF16))
            pv = _dot(vt_ref[0, hh, kj], p)
            if kj == 0:
                acc_sc[...] = pv
            else:
                acc_sc[...] = jnp.exp2(m - m_new) * acc_sc[...] + pv
            m = m_new
        acc = acc_sc[...]
        o_ref[0, hh, qi] = (acc[:MLA_V] * (1.0 / acc[MLA_V:MLA_V + 1])).astype(BF16)

    def q_tile_with_lookahead(item, carry):
        q_tile(item, is_last=False)
        return carry

    lax.fori_loop(0, n_items - 1, q_tile_with_lookahead, 0)
    q_tile(n_items - 1, is_last=True)


def _mla_attn(qt, k, vt):
    b, nh, nt, _, ts = qt.shape
    s = k.shape[2]
    nslots = min(MLA_SCORE_SLOTS, nt)
    assert nt % nslots == 0
    hps = MLA_HEADS_PER_STEP
    return pl.pallas_call(
        _mla_attn_kernel,
        grid=(b, nh // hps),
        in_specs=[
            pl.BlockSpec((1, hps, nt, MLA_QK, ts), lambda i, j: (i, j, 0, 0, 0)),
            pl.BlockSpec((1, hps, s, MLA_QK), lambda i, j: (i, j, 0, 0)),
            pl.BlockSpec((1, hps, nt, MLA_V + ONES_ROWS, ts), lambda i, j: (i, j, 0, 0, 0)),
        ],
        out_specs=pl.BlockSpec((1, hps, nt, MLA_V, ts), lambda i, j: (i, j, 0, 0, 0)),
        out_shape=jax.ShapeDtypeStruct((b, nh, nt, MLA_V, ts), BF16),
        scratch_shapes=[
            pltpu.VMEM((nslots, ts, ts), F32),
            pltpu.VMEM((nslots, 1, ts), F32),
            pltpu.VMEM((MLA_V + ONES_ROWS, ts), F32),
        ],
        compiler_params=_params(52 * 2**20, 2),
        name="mla_attn",
    )(qt, k, vt)


def _post_kernel(x_ref, mixt_ref, qmt_ref, sgt_ref, km_ref, vmt_ref, wo_ref, o_ref, s_sc):
    ts = x_ref.shape[1]
    for h in range(MEM_HEADS):
        lo = h * MEM_HEAD_DIM
        s_sc[h] = _dot(km_ref[0, h], qmt_ref[0, lo:lo + MEM_HEAD_DIM, :])
    sg = sgt_ref[0]
    bmix = mixt_ref[...].reshape(MIX_WIDTH, ts) * sg[:MIX_WIDTH]
    memo = []
    for h in range(MEM_HEADS):
        st = s_sc[h]
        p = jnp.exp2((st - jnp.max(st, axis=0, keepdims=True)).astype(BF16))
        pv = _dot(vmt_ref[0, h], p)
        memo.append(pv[:MEM_HEAD_DIM] * (1.0 / pv[MEM_HEAD_DIM:MEM_HEAD_DIM + 1]))
    bmem = (jnp.concatenate(memo, axis=0) * sg[MIX_WIDTH:].astype(F32)).astype(BF16)
    out = _dot_tn(bmix, wo_ref[:MIX_WIDTH]) + _dot_tn(bmem, wo_ref[MIX_WIDTH:])
    o_ref[0] = x_ref[0] + out


def _post(x, mixt, mixt_spec, qmt, sgt, km, vmt, wot, ts):
    b, s, _ = x.shape
    return pl.pallas_call(
        _post_kernel,
        grid=(b, s // ts),
        in_specs=[
            pl.BlockSpec((1, ts, D_MODEL), lambda i, j: (i, j, 0)),
            mixt_spec,
            pl.BlockSpec((1, MEM_WIDTH, ts), lambda i, j: (i, 0, j)),
            pl.BlockSpec((1, BRANCH_WIDTH, ts), lambda i, j: (i, 0, j)),
            pl.BlockSpec((1, MEM_HEADS, N_MEM, MEM_HEAD_DIM), lambda i, j: (i, 0, 0, 0)),
            pl.BlockSpec((1, MEM_HEADS, MEM_HEAD_DIM + ONES_ROWS, N_MEM), lambda i, j: (i, 0, 0, 0)),
            _full(wot.shape),
        ],
        out_specs=pl.BlockSpec((1, ts, D_MODEL), lambda i, j: (i, j, 0)),
        out_shape=jax.ShapeDtypeStruct(x.shape, x.dtype),
        scratch_shapes=[pltpu.VMEM((MEM_HEADS, N_MEM, ts), F32)],
        compiler_params=_params(48 * 2**20, 2),
        name="post",
    )(x, mixt, qmt, sgt, km, vmt, wot)


def _proj_b_kernel(x_ref, wbt_ref, gk_ref, qt_ref, k_ref, vt_ref, qmt_ref, sgt_ref):
    for tok in _sub_tiles(x_ref.shape[1], PROJ_B_SUB_TILE):
        h = _normed_input(x_ref, tok)
        _mem_query_and_gate(_dot_nt(wbt_ref[SWA_Q_W + 2 * SWA_KV_W:], h), qmt_ref, sgt_ref, tok)
        qt = _dot_nt(wbt_ref[:SWA_Q_W], h)
        for hh in range(SWA_HEADS):
            lo = hh * SWA_HEAD_DIM
            qh = qt[lo:lo + SWA_HEAD_DIM]
            qn = (qh * (_rms_scale_rows(qh) * SWA_Q_SCALE)).astype(BF16)
            g, hq = divmod(hh, SWA_GROUP)
            for c in range(qn.shape[1] // SWA_K_BLOCK):
                qt_ref[0, g, tok.start // SWA_K_BLOCK + c, :, hq * SWA_K_BLOCK:(hq + 1) * SWA_K_BLOCK] = (
                    qn[:, c * SWA_K_BLOCK:(c + 1) * SWA_K_BLOCK])
        kvt = _dot_nt(wbt_ref[SWA_Q_W:SWA_Q_W + 2 * SWA_KV_W], h)
        kparts = []
        for g in range(SWA_KV_HEADS):
            kh = kvt[g * SWA_HEAD_DIM:(g + 1) * SWA_HEAD_DIM]
            kparts.append(kh * _rms_scale_rows(kh))
        k_ref[0, tok, :] = (jnp.concatenate(kparts, axis=0).T * gk_ref[...]).astype(BF16)
        vt_ref[0, :, tok] = kvt[SWA_KV_W:].astype(BF16)


def _proj_b(x, wb, ts):
    b, s, _ = x.shape
    consts = [wb["wbt"], wb["gk"]]
    return pl.pallas_call(
        _proj_b_kernel,
        grid=(b, s // ts),
        in_specs=[pl.BlockSpec((1, ts, D_MODEL), lambda i, j: (i, j, 0))] + [_full(c.shape) for c in consts],
        out_specs=[
            pl.BlockSpec((1, SWA_KV_HEADS, ts // SWA_K_BLOCK, SWA_HEAD_DIM, SWA_GROUP * SWA_K_BLOCK),
                         lambda i, j: (i, 0, j, 0, 0)),
            pl.BlockSpec((1, ts, SWA_KV_W), lambda i, j: (i, j, 0)),
            pl.BlockSpec((1, SWA_KV_W, ts), lambda i, j: (i, 0, j)),
            pl.BlockSpec((1, MEM_WIDTH, ts), lambda i, j: (i, 0, j)),
            pl.BlockSpec((1, BRANCH_WIDTH, ts), lambda i, j: (i, 0, j)),
        ],
        out_shape=[
            jax.ShapeDtypeStruct((b, SWA_KV_HEADS, s // SWA_K_BLOCK, SWA_HEAD_DIM, SWA_GROUP * SWA_K_BLOCK), BF16),
            jax.ShapeDtypeStruct((b, s, SWA_KV_W), BF16),
            jax.ShapeDtypeStruct((b, SWA_KV_W, s), BF16),
            jax.ShapeDtypeStruct((b, MEM_WIDTH, s), BF16),
            jax.ShapeDtypeStruct((b, BRANCH_WIDTH, s), BF16),
        ],
        compiler_params=_params(56 * 2**20, 2),
        name="proj_b",
    )(x, *consts)


def _alibi_slope(h):
    return 2.0 ** (-8.0 * (h + 1) / SWA_HEADS)


def _swa_kernel(qt_ref, *refs):
    per_tile = qt_ref.shape[2]
    n_kblocks = per_tile + SWA_WINDOW_BLOCKS
    k_refs, refs = refs[:n_kblocks], refs[n_kblocks:]
    v_refs, refs = refs[:n_kblocks], refs[n_kblocks:]
    bias_lo_ref, bias_mid_ref, bias_hi_ref, sink_ref, o_ref, s_sc, mx_sc = refs
    k_all = jnp.concatenate([r[0] for r in k_refs], axis=0)
    vt_all = jnp.concatenate([r[0] for r in v_refs], axis=1)
    bias_refs = (bias_lo_ref,) + (bias_mid_ref,) * (per_tile - 2) + (bias_hi_ref,)
    chains = [(sb, g, c) for sb in range(per_tile) for g in range(SWA_KV_HEADS)
              for c in range(SWA_GROUP // SWA_HEADS_PER_CHAIN)]
    k_g, vt_g = {}, {}
    for sb in range(len(bias_refs)):
        k0 = sb * SWA_K_BLOCK
        for g in range(SWA_KV_HEADS):
            k_g[sb, g] = k_all[k0:k0 + SWA_BAND_KEYS, g * SWA_HEAD_DIM:(g + 1) * SWA_HEAD_DIM]
            vt_g[sb, g] = jnp.concatenate(
                [vt_all[g * SWA_HEAD_DIM:(g + 1) * SWA_HEAD_DIM, k0:k0 + SWA_BAND_KEYS],
                 jnp.ones((ONES_ROWS, SWA_BAND_KEYS), BF16)], axis=0)

    def put_scores(slot, chain):
        sb, g, c = chain
        lanes = slice(c * SWA_CHAIN_LANES, (c + 1) * SWA_CHAIN_LANES)
        st = _dot(k_g[sb, g], qt_ref[0, g, sb, :, lanes]) + bias_refs[sb][0, g, :, lanes]
        s_sc[slot] = st
        mx_sc[slot] = jnp.max(st, axis=0, keepdims=True)

    nslots = s_sc.shape[0]
    ahead = nslots - 1
    for t in range(min(ahead, len(chains))):
        put_scores(t, chains[t])
    for t, (sb, g, c) in enumerate(chains):
        if t + ahead < len(chains):
            put_scores((t + ahead) % nslots, chains[t + ahead])
        sink = sink_ref[g, :, c * SWA_CHAIN_LANES:(c + 1) * SWA_CHAIN_LANES]
        m = jnp.maximum(mx_sc[t % nslots], sink)
        p = jnp.exp2((s_sc[t % nslots] - m).astype(BF16))
        pv = _dot(vt_g[sb, g], p)
        l = pv[SWA_HEAD_DIM:SWA_HEAD_DIM + 1] + jnp.exp2(sink - m)
        ot = pv[:SWA_HEAD_DIM] * (1.0 / l)
        for hh in range(SWA_HEADS_PER_CHAIN):
            lo = ((g * SWA_GROUP) + c * SWA_HEADS_PER_CHAIN + hh) * SWA_HEAD_DIM
            o_ref[0, lo:lo + SWA_HEAD_DIM, sb * SWA_K_BLOCK:(sb + 1) * SWA_K_BLOCK] = (
                ot[:, hh * SWA_K_BLOCK:(hh + 1) * SWA_K_BLOCK].astype(BF16))


def _swa_attn(qt, k, vt, bias, sink):
    b, s = k.shape[0], k.shape[1]
    tq = min(SWA_Q_TILE, s)
    per_tile = tq // SWA_K_BLOCK
    assert per_tile >= 2
    n_kblocks = per_tile + SWA_WINDOW_BLOCKS
    last = s // SWA_K_BLOCK - 1
    last_tile = s // tq - 1

    def kblock(o):
        return lambda i, j: (i, jnp.clip(j * per_tile - 1 + o, 0, last), 0)

    def vblock(o):
        return lambda i, j: (i, 0, jnp.clip(j * per_tile - 1 + o, 0, last))

    bias_block = (1,) + bias.shape[1:]
    return pl.pallas_call(
        _swa_kernel,
        grid=(b, s // tq),
        in_specs=(
            [pl.BlockSpec((1, SWA_KV_HEADS, per_tile, SWA_HEAD_DIM, SWA_GROUP * SWA_K_BLOCK),
                          lambda i, j: (i, 0, j, 0, 0))]
            + [pl.BlockSpec((1, SWA_K_BLOCK, SWA_KV_W), kblock(o)) for o in range(n_kblocks)]
            + [pl.BlockSpec((1, SWA_KV_W, SWA_K_BLOCK), vblock(o)) for o in range(n_kblocks)]
            + [pl.BlockSpec(bias_block, lambda i, j: (jnp.where(j == 0, 0, 1), 0, 0, 0)),
               pl.BlockSpec(bias_block, lambda i, j: (1, 0, 0, 0)),
               pl.BlockSpec(bias_block, lambda i, j: (jnp.where(j == last_tile, 2, 1), 0, 0, 0)),
               _full(sink.shape)]
        ),
        out_specs=pl.BlockSpec((1, SWA_Q_W, tq), lambda i, j: (i, 0, j)),
        out_shape=jax.ShapeDtypeStruct((b, SWA_Q_W, s), BF16),
        scratch_shapes=[pltpu.VMEM((SWA_SCORE_SLOTS, SWA_BAND_KEYS, SWA_CHAIN_LANES), F32),
                        pltpu.VMEM((SWA_SCORE_SLOTS, 1, SWA_CHAIN_LANES), F32)],
        compiler_params=_params(48 * 2**20, 2),
        name="swa_attn",
    )(qt, *([k] * n_kblocks), *([vt] * n_kblocks), bias, bias, bias, sink)


def _col(v):
    return v.astype(F32)[:, None]


def _row(v):
    return v.astype(F32)[None, :]


def _prep_layer_a(seq, norm_in, a_w_in, a_q_a_norm, a_w_q_b, a_kv_a_norm, a_w_kv_b, a_q_norm, a_k_norm):
    kr_lo = MLA_Q_RANK + MLA_KV_RANK
    qm_lo = kr_lo + MLA_ROPE
    w_in = _col(norm_in) * a_w_in
    kr_cols = w_in[:, kr_lo:qm_lo]
    wkv = (_col(a_kv_a_norm) * a_w_kv_b).reshape(MLA_KV_RANK, MLA_HEADS, MLA_NOPE + MLA_V)
    inv = 1.0 / (ROPE_THETA ** (jnp.arange(0, MLA_ROPE, 2, dtype=F32) / MLA_ROPE))
    ang = jnp.arange(seq, dtype=F32)[:, None] * inv[None, :]
    cos, sin = jnp.cos(ang), jnp.sin(ang)
    g1 = _col(a_q_norm[MLA_NOPE:MLA_NOPE + MLA_HALF]) * MLA_Q_SCALE
    g2 = _col(a_q_norm[MLA_NOPE + MLA_HALF:]) * MLA_Q_SCALE
    return {
        "w1": jnp.concatenate([w_in[:, :kr_lo], kr_cols, kr_cols], axis=1).astype(BF16),
        "wgt": w_in[:, qm_lo:].T.astype(BF16),
        "wqbt": (_col(a_q_a_norm) * a_w_q_b).T.astype(BF16),
        "wkn": wkv[:, :, :MLA_NOPE].reshape(MLA_KV_RANK, MLA_HEADS * MLA_NOPE).astype(BF16),
        "wvt": wkv[:, :, MLA_NOPE:].reshape(MLA_KV_RANK, MLA_HEADS * MLA_V).T.astype(BF16),
        "qtab": jnp.concatenate([g1 * cos.T, g2 * sin.T, g2 * cos.T, g1 * sin.T], axis=0),
        "gkn": _row(a_k_norm[:MLA_NOPE] * a_q_norm[:MLA_NOPE]),
        "gkr": _row(jnp.tile(a_k_norm[MLA_NOPE:], 2)),
        "ctab": jnp.tile(cos, (1, 4)),
        "stab": jnp.concatenate([-sin, sin, -sin, sin], axis=1),
    }


def _prep_layer_b(norm_in, b_w_in, b_q_norm, b_k_norm):
    return {
        "wbt": (_col(norm_in) * b_w_in).T.astype(BF16),
        "gk": _row(jnp.tile(b_k_norm * b_q_norm, SWA_KV_HEADS)),
    }


def _prep_mem(mem_norm, w_mem_kv, mem_k_norm, mem_q_norm):
    w = _col(mem_norm) * w_mem_kv
    return {
        "wk": w[:, :MEM_WIDTH].astype(BF16),
        "wvt": w[:, MEM_WIDTH:].T.astype(BF16),
        "gk": _row(mem_k_norm * mem_q_norm),
    }


def _swa_bias():
    r = jnp.arange(SWA_BAND_KEYS)[:, None]
    c = jnp.arange(SWA_K_BLOCK)[None, :]
    rel = r - WINDOW - c
    dist = jnp.abs(rel).astype(F32)
    slopes = jnp.array([_alibi_slope(h) for h in range(SWA_HEADS)], F32).reshape(SWA_KV_HEADS, SWA_GROUP)
    alibi = -(LOG2E * slopes)[:, None, :, None] * dist[None, :, None, :]
    in_window = (jnp.abs(rel) <= WINDOW)[None, None, :, None, :]
    key_block = (r // SWA_K_BLOCK)[None, None, :, None, :]
    variant = jnp.arange(3)[:, None, None, None, None]
    in_seq = ~(((variant == 0) & (key_block == 0)) | ((variant == 2) & (key_block == 2)))
    bias = jnp.where(in_window & in_seq, alibi[None], NEG)
    return bias.reshape(3, SWA_KV_HEADS, SWA_BAND_KEYS, SWA_GROUP * SWA_K_BLOCK)


def _swa_sink_lanes(b_sink):
    s = (b_sink.astype(F32) * LOG2E).reshape(SWA_KV_HEADS, 1, SWA_GROUP, 1)
    return jnp.broadcast_to(s, (SWA_KV_HEADS, 1, SWA_GROUP, SWA_K_BLOCK)).reshape(SWA_KV_HEADS, 1, -1)


def _trunk(x, mem, wa, wb, wmem, wots, sink, bias):
    ts = min(SEQ_TILE, x.shape[1])
    b = x.shape[0]
    nt = x.shape[1] // ts

    km, vmt = _memkv(mem, **wmem[0])
    qt, k, vt, qmt, sgt = _proj_a(x, wa, ts)
    mixt = _mla_attn(qt, k, vt)
    mix_spec = pl.BlockSpec((1, MLA_HEADS, 1, MLA_V, ts), lambda i, j: (i, 0, j, 0, 0))
    x = _post(x, mixt, mix_spec, qmt, sgt, km, vmt, wots[0], ts)

    km, vmt = _memkv(mem, **wmem[1])
    qt, k, vt, qmt, sgt = _proj_b(x, wb, ts)
    mixt = _swa_attn(qt, k, vt, bias, sink)
    mix_spec = pl.BlockSpec((1, MIX_WIDTH, ts), lambda i, j: (i, 0, j))
    return _post(x, mixt, mix_spec, qmt, sgt, km, vmt, wots[1], ts)


def kernel(x_prompt, x_sample, mem_prompt, mem_sample, norm_in, w_out, mem_norm, w_mem_kv, mem_q_norm, mem_k_norm, a_w_in, a_q_a_norm, a_w_q_b, a_kv_a_norm, a_w_kv_b, a_q_norm, a_k_norm, b_w_in, b_q_norm, b_k_norm, b_sink):
    assert norm_in.shape[0] == 2 and a_w_in.shape[0] == 1 and b_w_in.shape[0] == 1
    assert x_prompt.shape[1] == x_sample.shape[1]
    seq = x_prompt.shape[1]
    assert seq % min(SWA_Q_TILE, seq) == 0 and seq % min(SEQ_TILE, seq) == 0
    wa = _prep_layer_a(seq, norm_in[0], a_w_in[0], a_q_a_norm[0], a_w_q_b[0], a_kv_a_norm[0], a_w_kv_b[0],
                       a_q_norm[0], a_k_norm[0])
    wb = _prep_layer_b(norm_in[1], b_w_in[0], b_q_norm[0], b_k_norm[0])
    wmem = [_prep_mem(mem_norm[i], w_mem_kv[i], mem_k_norm[i], mem_q_norm[i]) for i in range(2)]
    wots = [w_out[i].astype(BF16) for i in range(2)]
    sink = _swa_sink_lanes(b_sink[0])
    bias = _swa_bias()
    y_prompt = _trunk(x_prompt, mem_prompt, wa, wb, wmem, wots, sink, bias)
    y_sample = _trunk(x_sample, mem_sample, wa, wb, wmem, wots, sink, bias)
    return (y_prompt, y_sample)
```

```python
import functools
import math

import jax
import jax.numpy as jnp
import numpy as np
from jax import lax
from jax.experimental import pallas as pl
from jax.experimental.pallas import tpu as pltpu

D_MODEL = 1024
N_MEM = 256
MEM_HEADS = 4
MEM_HEAD_DIM = 128
MEM_WIDTH = MEM_HEADS * MEM_HEAD_DIM

MLA_HEADS = 8
MLA_Q_RANK = 384
MLA_KV_RANK = 256
MLA_NOPE = 128
MLA_ROPE = 64
MLA_HALF = MLA_ROPE // 2
MLA_V = 128
MLA_QK = MLA_NOPE + MLA_ROPE
ROPE_THETA = 10000.0

SWA_HEADS = 16
SWA_KV_HEADS = 2
SWA_HEAD_DIM = 64
SWA_GROUP = SWA_HEADS // SWA_KV_HEADS
SWA_Q_W = SWA_HEADS * SWA_HEAD_DIM
SWA_KV_W = SWA_KV_HEADS * SWA_HEAD_DIM
WINDOW = 128

MIX_WIDTH = 1024
BRANCH_WIDTH = MIX_WIDTH + MEM_WIDTH
EPS = 1e-6
NEG = -1e30

V7X_LANES = 128
V7X_VMEM_BYTES = 64 * 1024 * 1024

SEQ_TILE = 512
PROJ_A_SUB_TILE = 512
PROJ_B_TILE = 1024
PROJ_B_SUB_TILE = 256
SWA_Q_TILE = 512
SWA_K_BLOCK = 128
SWA_WINDOW_BLOCKS = 2 * WINDOW // SWA_K_BLOCK
SWA_BAND_KEYS = SWA_K_BLOCK + 2 * WINDOW
SWA_HEADS_PER_CHAIN = 2
SWA_CHAIN_LANES = SWA_HEADS_PER_CHAIN * SWA_K_BLOCK
MLA_SCORE_SLOTS = 8
MLA_HEADS_PER_STEP = 2
SWA_SCORE_SLOTS = 4
LOG2E = math.log2(math.e)
MLA_Q_SCALE = MLA_QK ** -0.5 * LOG2E
SWA_Q_SCALE = SWA_HEAD_DIM ** -0.5 * LOG2E
MEM_Q_SCALE = MEM_HEAD_DIM ** -0.5 * LOG2E
ONES_ROWS = 16

F32 = jnp.float32
BF16 = jnp.bfloat16

_NT = (((1,), (1,)), ((), ()))
_TN = (((0,), (0,)), ((), ()))


def _vmem_limit(nbytes):
    return int(min(nbytes, V7X_VMEM_BYTES - 8 * 1024 * 1024))


def _params(nbytes, ndims):
    return pltpu.CompilerParams(
        dimension_semantics=("arbitrary",) * ndims,
        vmem_limit_bytes=_vmem_limit(nbytes))


def _dot(a, b):
    return jnp.dot(a, b, preferred_element_type=F32)


def _dot_nt(a, b):
    return lax.dot_general(a, b, _NT, preferred_element_type=F32)


def _dot_tn(a, b):
    return lax.dot_general(a, b, _TN, preferred_element_type=F32)


def _rms_lanes(x):
    return x * lax.rsqrt(jnp.mean(x * x, axis=-1, keepdims=True) + EPS)


def _rms_scale_rows(x):
    return lax.rsqrt(jnp.mean(x * x, axis=0, keepdims=True) + EPS)


def _full(shape):
    return pl.BlockSpec(shape, lambda *_: (0,) * len(shape))


def _memkv_kernel(mem_ref, wk_ref, wvt_ref, gk_ref, km_ref, vmt_ref):
    mn = _rms_lanes(mem_ref[0]).astype(BF16)
    k = _dot(mn, wk_ref[...])
    vt = _dot_nt(wvt_ref[...], mn)
    for h in range(MEM_HEADS):
        lo = h * MEM_HEAD_DIM
        kh = k[:, lo:lo + MEM_HEAD_DIM]
        km_ref[0, h] = (_rms_lanes(kh) * gk_ref[...]).astype(BF16)
        vmt_ref[0, h, :MEM_HEAD_DIM, :] = vt[lo:lo + MEM_HEAD_DIM].astype(BF16)
        vmt_ref[0, h, MEM_HEAD_DIM:, :] = jnp.ones((ONES_ROWS, N_MEM), BF16)


def _memkv(mem, wk, wvt, gk):
    b = mem.shape[0]
    return pl.pallas_call(
        _memkv_kernel,
        grid=(b,),
        in_specs=[
            pl.BlockSpec((1, N_MEM, D_MODEL), lambda i: (i, 0, 0)),
            _full((D_MODEL, MEM_WIDTH)),
            _full((MEM_WIDTH, D_MODEL)),
            _full((1, MEM_HEAD_DIM)),
        ],
        out_specs=[
            pl.BlockSpec((1, MEM_HEADS, N_MEM, MEM_HEAD_DIM), lambda i: (i, 0, 0, 0)),
            pl.BlockSpec((1, MEM_HEADS, MEM_HEAD_DIM + ONES_ROWS, N_MEM), lambda i: (i, 0, 0, 0)),
        ],
        out_shape=[
            jax.ShapeDtypeStruct((b, MEM_HEADS, N_MEM, MEM_HEAD_DIM), BF16),
            jax.ShapeDtypeStruct((b, MEM_HEADS, MEM_HEAD_DIM + ONES_ROWS, N_MEM), BF16),
        ],
        compiler_params=_params(24 * 2**20, 1),
        name="memkv",
    )(mem, wk, wvt, gk)


def _normed_input(x_ref, tok):
    return _rms_lanes(x_ref[0, tok, :]).astype(BF16)


def _mem_query_and_gate(zt, qmt_ref, sgt_ref, tok):
    for h in range(MEM_HEADS):
        lo = h * MEM_HEAD_DIM
        qm = zt[lo:lo + MEM_HEAD_DIM]
        qmt_ref[0, lo:lo + MEM_HEAD_DIM, tok] = (qm * (_rms_scale_rows(qm) * MEM_Q_SCALE)).astype(BF16)
    hg = 0.5 * zt[MEM_WIDTH:]
    sgt_ref[0, :, tok] = (hg + hg * jnp.tanh(hg)).astype(BF16)


def _sub_tiles(ts, sub):
    sub = min(sub, ts)
    return [slice(c * sub, (c + 1) * sub) for c in range(ts // sub)]


def _proj_a_kernel(x_ref, w1_ref, wgt_ref, wqbt_ref, wkn_ref, wvt_ref,
                   qtab_ref, gkn_ref, gkr_ref, ctab_ref, stab_ref,
                   qt_ref, k_ref, vt_ref, qmt_ref, sgt_ref):
    kv_lo = MLA_Q_RANK
    kr_lo = MLA_Q_RANK + MLA_KV_RANK
    for tok in _sub_tiles(x_ref.shape[1], PROJ_A_SUB_TILE):
        h = _normed_input(x_ref, tok)
        z1 = _dot(h, w1_ref[...])
        _mem_query_and_gate(_dot_nt(wgt_ref[...], h), qmt_ref, sgt_ref, tok)
        cqn = _rms_lanes(z1[:, :kv_lo]).astype(BF16)
        ckvn = _rms_lanes(z1[:, kv_lo:kr_lo]).astype(BF16)
        kr = z1[:, kr_lo:]

        qt = _dot_nt(wqbt_ref[...], cqn)
        a1 = qtab_ref[0 * MLA_HALF:1 * MLA_HALF, tok]
        b1 = qtab_ref[1 * MLA_HALF:2 * MLA_HALF, tok]
        a2 = qtab_ref[2 * MLA_HALF:3 * MLA_HALF, tok]
        b2 = qtab_ref[3 * MLA_HALF:4 * MLA_HALF, tok]
        for hh in range(MLA_HEADS):
            qh = qt[hh * MLA_QK:(hh + 1) * MLA_QK]
            r = _rms_scale_rows(qh)
            x1 = qh[MLA_NOPE:MLA_NOPE + MLA_HALF]
            x2 = qh[MLA_NOPE + MLA_HALF:]
            qt_ref[0, hh, 0, :MLA_NOPE, tok] = (qh[:MLA_NOPE] * (r * MLA_Q_SCALE)).astype(BF16)
            qt_ref[0, hh, 0, MLA_NOPE:MLA_NOPE + MLA_HALF, tok] = ((x1 * a1 - x2 * b1) * r).astype(BF16)
            qt_ref[0, hh, 0, MLA_NOPE + MLA_HALF:, tok] = ((x2 * a2 + x1 * b2) * r).astype(BF16)

        kn = _dot(ckvn, wkn_ref[...])
        vt = _dot_nt(wvt_ref[...], ckvn)
        krg = kr * gkr_ref[...]
        kro = krg * ctab_ref[tok, :] + pltpu.roll(krg, MLA_HALF, 1) * stab_ref[tok, :]
        ss_r = 0.5 * jnp.sum(kr * kr, axis=1, keepdims=True)
        for hh in range(MLA_HEADS):
            knh = kn[:, hh * MLA_NOPE:(hh + 1) * MLA_NOPE]
            ss = jnp.sum(knh * knh, axis=1, keepdims=True) + ss_r
            r = lax.rsqrt(ss * (1.0 / MLA_QK) + EPS)
            k_ref[0, hh, tok, :MLA_NOPE] = (knh * gkn_ref[...] * r).astype(BF16)
            k_ref[0, hh, tok, MLA_NOPE:] = (kro[:, :MLA_ROPE] * r).astype(BF16)
            vt_ref[0, hh, 0, :MLA_V, tok] = vt[hh * MLA_V:(hh + 1) * MLA_V].astype(BF16)
            vt_ref[0, hh, 0, MLA_V:, tok] = jnp.ones((ONES_ROWS, vt.shape[1]), BF16)


def _proj_a(x, wa, ts):
    b, s, _ = x.shape
    nt = s // ts
    consts = [wa["w1"], wa["wgt"], wa["wqbt"], wa["wkn"], wa["wvt"]]
    tail = [wa["gkn"], wa["gkr"]]
    in_specs = (
        [pl.BlockSpec((1, ts, D_MODEL), lambda i, j: (i, j, 0))]
        + [_full(c.shape) for c in consts]
        + [pl.BlockSpec((4 * MLA_HALF, ts), lambda i, j: (0, j))]
        + [_full(c.shape) for c in tail]
        + [pl.BlockSpec((ts, V7X_LANES), lambda i, j: (j, 0))] * 2
    )
    out_shape = [
        jax.ShapeDtypeStruct((b, MLA_HEADS, nt, MLA_QK, ts), BF16),
        jax.ShapeDtypeStruct((b, MLA_HEADS, s, MLA_QK), BF16),
        jax.ShapeDtypeStruct((b, MLA_HEADS, nt, MLA_V + ONES_ROWS, ts), BF16),
        jax.ShapeDtypeStruct((b, MEM_WIDTH, s), BF16),
        jax.ShapeDtypeStruct((b, BRANCH_WIDTH, s), BF16),
    ]
    out_specs = [
        pl.BlockSpec((1, MLA_HEADS, 1, MLA_QK, ts), lambda i, j: (i, 0, j, 0, 0)),
        pl.BlockSpec((1, MLA_HEADS, ts, MLA_QK), lambda i, j: (i, 0, j, 0)),
        pl.BlockSpec((1, MLA_HEADS, 1, MLA_V + ONES_ROWS, ts), lambda i, j: (i, 0, j, 0, 0)),
        pl.BlockSpec((1, MEM_WIDTH, ts), lambda i, j: (i, 0, j)),
        pl.BlockSpec((1, BRANCH_WIDTH, ts), lambda i, j: (i, 0, j)),
    ]
    return pl.pallas_call(
        _proj_a_kernel,
        grid=(b, nt),
        in_specs=in_specs,
        out_specs=out_specs,
        out_shape=out_shape,
        compiler_params=_params(56 * 2**20, 2),
        name="proj_a",
    )(x, *consts, wa["qtab"], *tail, wa["ctab"], wa["stab"])


def _mla_attn_kernel(qt_ref, k_ref, vt_ref, o_ref, s_sc, mx_sc, acc_sc):
    nq = qt_ref.shape[2]
    nk = vt_ref.shape[2]
    tk = vt_ref.shape[4]
    tq = qt_ref.shape[4]
    nslots = s_sc.shape[0]
    ahead = nslots - 1

    n_items = qt_ref.shape[1] * nq

    def put_scores(item, kj):
        hh, qi = item // nq, item % nq
        st = _dot(k_ref[0, hh, kj * tk:(kj + 1) * tk, :], qt_ref[0, hh, qi])
        s_sc[kj % nslots] = st
        mx_sc[kj % nslots] = jnp.max(st, axis=0, keepdims=True)

    for kj in range(ahead):
        put_scores(0, kj)

    def q_tile(item, is_last):
        hh, qi = item // nq, item % nq
        m = jnp.full((1, tq), NEG, F32)
        for kj in range(nk):
            cur = kj % nslots
            if kj + ahead < nk:
                put_scores(item, kj + ahead)
            elif not is_last:
                put_scores(item + 1, kj + ahead - nk)
            m_new = jnp.maximum(m, mx_sc[cur])
            p = jnp.exp2((s_sc[cur] - m_new).astype(BF16))
            pv = _dot(vt_ref[0, hh, kj], p)
            if kj == 0:
                acc_sc[...] = pv
            else:
                acc_sc[...] = jnp.exp2(m - m_new) * acc_sc[...] + pv
            m = m_new
        acc = acc_sc[...]
        o_ref[0, hh, qi] = (acc[:MLA_V] * (1.0 / acc[MLA_V:MLA_V + 1])).astype(BF16)

    def q_tile_with_lookahead(item, carry):
        q_tile(item, is_last=False)
        return carry

    lax.fori_loop(0, n_items - 1, q_tile_with_lookahead, 0)
    q_tile(n_items - 1, is_last=True)


def _mla_attn(qt, k, vt):
    b, nh, nt, _, ts = qt.shape
    s = k.shape[2]
    nslots = min(MLA_SCORE_SLOTS, nt)
    assert nt % nslots == 0
    hps = MLA_HEADS_PER_STEP
    return pl.pallas_call(
        _mla_attn_kernel,
        grid=(b, nh // hps),
        in_specs=[
            pl.BlockSpec((1, hps, nt, MLA_QK, ts), lambda i, j: (i, j, 0, 0, 0)),
            pl.BlockSpec((1, hps, s, MLA_QK), lambda i, j: (i, j, 0, 0)),
            pl.BlockSpec((1, hps, nt, MLA_V + ONES_ROWS, ts), lambda i, j: (i, j, 0, 0, 0)),
        ],
        out_specs=pl.BlockSpec((1, hps, nt, MLA_V, ts), lambda i, j: (i, j, 0, 0, 0)),
        out_shape=jax.ShapeDtypeStruct((b, nh, nt, MLA_V, ts), BF16),
        scratch_shapes=[
            pltpu.VMEM((nslots, ts, ts), F32),
            pltpu.VMEM((nslots, 1, ts), F32),
            pltpu.VMEM((MLA_V + ONES_ROWS, ts), F32),
        ],
        compiler_params=_params(52 * 2**20, 2),
        name="mla_attn",
    )(qt, k, vt)


def _post_kernel(x_ref, mixt_ref, qmt_ref, sgt_ref, km_ref, vmt_ref, wo_ref, o_ref, s_sc):
    ts = x_ref.shape[1]
    for h in range(MEM_HEADS):
        lo = h * MEM_HEAD_DIM
        s_sc[h] = _dot(km_ref[0, h], qmt_ref[0, lo:lo + MEM_HEAD_DIM, :])
    sg = sgt_ref[0]
    bmix = mixt_ref[...].reshape(MIX_WIDTH, ts) * sg[:MIX_WIDTH]
    memo = []
    for h in range(MEM_HEADS):
        st = s_sc[h]
        p = jnp.exp2((st - jnp.max(st, axis=0, keepdims=True)).astype(BF16))
        pv = _dot(vmt_ref[0, h], p)
        memo.append(pv[:MEM_HEAD_DIM] * (1.0 / pv[MEM_HEAD_DIM:MEM_HEAD_DIM + 1]))
    bmem = (jnp.concatenate(memo, axis=0) * sg[MIX_WIDTH:].astype(F32)).astype(BF16)
    out = _dot_tn(bmix, wo_ref[:MIX_WIDTH]) + _dot_tn(bmem, wo_ref[MIX_WIDTH:])
    o_ref[0] = x_ref[0] + out


def _post(x, mixt, mixt_spec, qmt, sgt, km, vmt, wot, ts):
    b, s, _ = x.shape
    return pl.pallas_call(
        _post_kernel,
        grid=(b, s // ts),
        in_specs=[
            pl.BlockSpec((1, ts, D_MODEL), lambda i, j: (i, j, 0)),
            mixt_spec,
            pl.BlockSpec((1, MEM_WIDTH, ts), lambda i, j: (i, 0, j)),
            pl.BlockSpec((1, BRANCH_WIDTH, ts), lambda i, j: (i, 0, j)),
            pl.BlockSpec((1, MEM_HEADS, N_MEM, MEM_HEAD_DIM), lambda i, j: (i, 0, 0, 0)),
            pl.BlockSpec((1, MEM_HEADS, MEM_HEAD_DIM + ONES_ROWS, N_MEM), lambda i, j: (i, 0, 0, 0)),
            _full(wot.shape),
        ],
        out_specs=pl.BlockSpec((1, ts, D_MODEL), lambda i, j: (i, j, 0)),
        out_shape=jax.ShapeDtypeStruct(x.shape, x.dtype),
        scratch_shapes=[pltpu.VMEM((MEM_HEADS, N_MEM, ts), F32)],
        compiler_params=_params(48 * 2**20, 2),
        name="post",
    )(x, mixt, qmt, sgt, km, vmt, wot)


def _proj_b_kernel(x_ref, wbt_ref, gk_ref, qt_ref, k_ref, vt_ref, qmt_ref, sgt_ref):
    for tok in _sub_tiles(x_ref.shape[1], PROJ_B_SUB_TILE):
        h = _normed_input(x_ref, tok)
        _mem_query_and_gate(_dot_nt(wbt_ref[SWA_Q_W + 2 * SWA_KV_W:], h), qmt_ref, sgt_ref, tok)
        qt = _dot_nt(wbt_ref[:SWA_Q_W], h)
        for hh in range(SWA_HEADS):
            lo = hh * SWA_HEAD_DIM
            qh = qt[lo:lo + SWA_HEAD_DIM]
            qn = (qh * (_rms_scale_rows(qh) * SWA_Q_SCALE)).astype(BF16)
            g, hq = divmod(hh, SWA_GROUP)
            for c in range(qn.shape[1] // SWA_K_BLOCK):
                qt_ref[0, g, tok.start // SWA_K_BLOCK + c, :, hq * SWA_K_BLOCK:(hq + 1) * SWA_K_BLOCK] = (
                    qn[:, c * SWA_K_BLOCK:(c + 1) * SWA_K_BLOCK])
        kvt = _dot_nt(wbt_ref[SWA_Q_W:SWA_Q_W + 2 * SWA_KV_W], h)
        kparts = []
        for g in range(SWA_KV_HEADS):
            kh = kvt[g * SWA_HEAD_DIM:(g + 1) * SWA_HEAD_DIM]
            kparts.append(kh * _rms_scale_rows(kh))
        k_ref[0, tok, :] = (jnp.concatenate(kparts, axis=0).T * gk_ref[...]).astype(BF16)
        vt = kvt[SWA_KV_W:].astype(BF16)
        for c in range(vt.shape[1] // SWA_K_BLOCK):
            vt_ref[0, tok.start // SWA_K_BLOCK + c] = vt[:, c * SWA_K_BLOCK:(c + 1) * SWA_K_BLOCK]


def _proj_b(x, wb, ts):
    b, s, _ = x.shape
    consts = [wb["wbt"], wb["gk"]]
    return pl.pallas_call(
        _proj_b_kernel,
        grid=(b, s // ts),
        in_specs=[pl.BlockSpec((1, ts, D_MODEL), lambda i, j: (i, j, 0))] + [_full(c.shape) for c in consts],
        out_specs=[
            pl.BlockSpec((1, SWA_KV_HEADS, ts // SWA_K_BLOCK, SWA_HEAD_DIM, SWA_GROUP * SWA_K_BLOCK),
                         lambda i, j: (i, 0, j, 0, 0)),
            pl.BlockSpec((1, ts, SWA_KV_W), lambda i, j: (i, j, 0)),
            pl.BlockSpec((1, ts // SWA_K_BLOCK, SWA_KV_W, SWA_K_BLOCK), lambda i, j: (i, j, 0, 0)),
            pl.BlockSpec((1, MEM_WIDTH, ts), lambda i, j: (i, 0, j)),
            pl.BlockSpec((1, BRANCH_WIDTH, ts), lambda i, j: (i, 0, j)),
        ],
        out_shape=[
            jax.ShapeDtypeStruct((b, SWA_KV_HEADS, s // SWA_K_BLOCK, SWA_HEAD_DIM, SWA_GROUP * SWA_K_BLOCK), BF16),
            jax.ShapeDtypeStruct((b, s, SWA_KV_W), BF16),
            jax.ShapeDtypeStruct((b, s // SWA_K_BLOCK, SWA_KV_W, SWA_K_BLOCK), BF16),
            jax.ShapeDtypeStruct((b, MEM_WIDTH, s), BF16),
            jax.ShapeDtypeStruct((b, BRANCH_WIDTH, s), BF16),
        ],
        compiler_params=_params(56 * 2**20, 2),
        name="proj_b",
    )(x, *consts)


def _alibi_slope(h):
    return 2.0 ** (-8.0 * (h + 1) / SWA_HEADS)


def _swa_kernel(qt_ref, k_ref, vt_ref, bias_lo_ref, bias_mid_ref, bias_hi_ref, sink_ref, o_ref, s_sc, mx_sc):
    per_tile = qt_ref.shape[2]
    n_blocks = vt_ref.shape[1]
    first = pl.program_id(1) * per_tile - SWA_WINDOW_BLOCKS // 2
    blocks = [jnp.clip(first + o, 0, n_blocks - 1) for o in range(per_tile + SWA_WINDOW_BLOCKS)]
    k_all = jnp.concatenate(
        [k_ref[0, pl.ds(pl.multiple_of(blk * SWA_K_BLOCK, SWA_K_BLOCK), SWA_K_BLOCK), :] for blk in blocks],
        axis=0)
    vt_all = jnp.concatenate([vt_ref[0, blk] for blk in blocks], axis=1)
    bias_refs = (bias_lo_ref,) + (bias_mid_ref,) * (per_tile - 2) + (bias_hi_ref,)
    chains = [(sb, g, c) for sb in range(per_tile) for g in range(SWA_KV_HEADS)
              for c in range(SWA_GROUP // SWA_HEADS_PER_CHAIN)]
    k_g, vt_g = {}, {}
    for sb in range(len(bias_refs)):
        k0 = sb * SWA_K_BLOCK
        for g in range(SWA_KV_HEADS):
            k_g[sb, g] = k_all[k0:k0 + SWA_BAND_KEYS, g * SWA_HEAD_DIM:(g + 1) * SWA_HEAD_DIM]
            vt_g[sb, g] = jnp.concatenate(
                [vt_all[g * SWA_HEAD_DIM:(g + 1) * SWA_HEAD_DIM, k0:k0 + SWA_BAND_KEYS],
                 jnp.ones((ONES_ROWS, SWA_BAND_KEYS), BF16)], axis=0)

    def put_scores(slot, chain):
        sb, g, c = chain
        lanes = slice(c * SWA_CHAIN_LANES, (c + 1) * SWA_CHAIN_LANES)
        st = _dot(k_g[sb, g], qt_ref[0, g, sb, :, lanes]) + bias_refs[sb][0, g, :, lanes]
        s_sc[slot] = st
        mx_sc[slot] = jnp.max(st, axis=0, keepdims=True)

    nslots = s_sc.shape[0]
    ahead = nslots - 1
    for t in range(min(ahead, len(chains))):
        put_scores(t, chains[t])
    for t, (sb, g, c) in enumerate(chains):
        if t + ahead < len(chains):
            put_scores((t + ahead) % nslots, chains[t + ahead])
        sink = sink_ref[g, :, c * SWA_CHAIN_LANES:(c + 1) * SWA_CHAIN_LANES]
        m = jnp.maximum(mx_sc[t % nslots], sink)
        p = jnp.exp2((s_sc[t % nslots] - m).astype(BF16))
        pv = _dot(vt_g[sb, g], p)
        l = pv[SWA_HEAD_DIM:SWA_HEAD_DIM + 1] + jnp.exp2(sink - m)
        ot = pv[:SWA_HEAD_DIM] * (1.0 / l)
        for hh in range(SWA_HEADS_PER_CHAIN):
            lo = ((g * SWA_GROUP) + c * SWA_HEADS_PER_CHAIN + hh) * SWA_HEAD_DIM
            o_ref[0, lo:lo + SWA_HEAD_DIM, sb * SWA_K_BLOCK:(sb + 1) * SWA_K_BLOCK] = (
                ot[:, hh * SWA_K_BLOCK:(hh + 1) * SWA_K_BLOCK].astype(BF16))


def _swa_attn(qt, k, vt, bias, sink):
    b, s = k.shape[0], k.shape[1]
    tq = min(SWA_Q_TILE, s)
    per_tile = tq // SWA_K_BLOCK
    assert per_tile >= 2
    last_tile = s // tq - 1
    bias_block = (1,) + bias.shape[1:]
    return pl.pallas_call(
        _swa_kernel,
        grid=(b, s // tq),
        in_specs=(
            [pl.BlockSpec((1, SWA_KV_HEADS, per_tile, SWA_HEAD_DIM, SWA_GROUP * SWA_K_BLOCK),
                          lambda i, j: (i, 0, j, 0, 0))]
            + [pl.BlockSpec((1,) + k.shape[1:], lambda i, j: (i, 0, 0)),
               pl.BlockSpec((1,) + vt.shape[1:], lambda i, j: (i, 0, 0, 0))]
            + [pl.BlockSpec(bias_block, lambda i, j: (jnp.where(j == 0, 0, 1), 0, 0, 0)),
               pl.BlockSpec(bias_block, lambda i, j: (1, 0, 0, 0)),
               pl.BlockSpec(bias_block, lambda i, j: (jnp.where(j == last_tile, 2, 1), 0, 0, 0)),
               _full(sink.shape)]
        ),
        out_specs=pl.BlockSpec((1, SWA_Q_W, tq), lambda i, j: (i, 0, j)),
        out_shape=jax.ShapeDtypeStruct((b, SWA_Q_W, s), BF16),
        scratch_shapes=[pltpu.VMEM((SWA_SCORE_SLOTS, SWA_BAND_KEYS, SWA_CHAIN_LANES), F32),
                        pltpu.VMEM((SWA_SCORE_SLOTS, 1, SWA_CHAIN_LANES), F32)],
        compiler_params=_params(48 * 2**20, 2),
        name="swa_attn",
    )(qt, k, vt, bias, bias, bias, sink)


def _col(v):
    return v.astype(F32)[:, None]


def _row(v):
    return v.astype(F32)[None, :]


def _prep_layer_a(seq, norm_in, a_w_in, a_q_a_norm, a_w_q_b, a_kv_a_norm, a_w_kv_b, a_q_norm, a_k_norm):
    kr_lo = MLA_Q_RANK + MLA_KV_RANK
    qm_lo = kr_lo + MLA_ROPE
    w_in = _col(norm_in) * a_w_in
    kr_cols = w_in[:, kr_lo:qm_lo]
    wkv = (_col(a_kv_a_norm) * a_w_kv_b).reshape(MLA_KV_RANK, MLA_HEADS, MLA_NOPE + MLA_V)
    inv = 1.0 / (ROPE_THETA ** (np.arange(0, MLA_ROPE, 2, dtype=np.float64) / MLA_ROPE))
    ang = np.arange(seq, dtype=np.float64)[:, None] * inv[None, :]
    cos, sin = np.cos(ang).astype(np.float32), np.sin(ang).astype(np.float32)
    g1 = _col(a_q_norm[MLA_NOPE:MLA_NOPE + MLA_HALF]) * MLA_Q_SCALE
    g2 = _col(a_q_norm[MLA_NOPE + MLA_HALF:]) * MLA_Q_SCALE
    return {
        "w1": jnp.concatenate([w_in[:, :kr_lo], kr_cols, kr_cols], axis=1).astype(BF16),
        "wgt": w_in[:, qm_lo:].T.astype(BF16),
        "wqbt": (_col(a_q_a_norm) * a_w_q_b).T.astype(BF16),
        "wkn": wkv[:, :, :MLA_NOPE].reshape(MLA_KV_RANK, MLA_HEADS * MLA_NOPE).astype(BF16),
        "wvt": wkv[:, :, MLA_NOPE:].reshape(MLA_KV_RANK, MLA_HEADS * MLA_V).T.astype(BF16),
        "qtab": jnp.concatenate([g1 * cos.T, g2 * sin.T, g2 * cos.T, g1 * sin.T], axis=0),
        "gkn": _row(a_k_norm[:MLA_NOPE] * a_q_norm[:MLA_NOPE]),
        "gkr": _row(jnp.tile(a_k_norm[MLA_NOPE:], 2)),
        "ctab": jnp.asarray(np.tile(cos, (1, 4))),
        "stab": jnp.asarray(np.concatenate([-sin, sin, -sin, sin], axis=1)),
    }


def _prep_layer_b(norm_in, b_w_in, b_q_norm, b_k_norm):
    return {
        "wbt": (_col(norm_in) * b_w_in).T.astype(BF16),
        "gk": _row(jnp.tile(b_k_norm * b_q_norm, SWA_KV_HEADS)),
    }


def _prep_mem(mem_norm, w_mem_kv, mem_k_norm, mem_q_norm):
    w = _col(mem_norm) * w_mem_kv
    return {
        "wk": w[:, :MEM_WIDTH].astype(BF16),
        "wvt": w[:, MEM_WIDTH:].T.astype(BF16),
        "gk": _row(mem_k_norm * mem_q_norm),
    }


def _swa_bias():
    r = np.arange(SWA_BAND_KEYS)[:, None]
    c = np.arange(SWA_K_BLOCK)[None, :]
    rel = r - WINDOW - c
    dist = np.abs(rel).astype(np.float64)
    slopes = np.array([_alibi_slope(h) for h in range(SWA_HEADS)]).reshape(SWA_KV_HEADS, SWA_GROUP)
    alibi = -(LOG2E * slopes)[:, None, :, None] * dist[None, :, None, :]
    in_window = (np.abs(rel) <= WINDOW)[None, None, :, None, :]
    key_block = (r // SWA_K_BLOCK)[None, None, :, None, :]
    variant = np.arange(3)[:, None, None, None, None]
    in_seq = ~(((variant == 0) & (key_block == 0)) | ((variant == 2) & (key_block == 2)))
    bias = np.where(in_window & in_seq, alibi[None], NEG).astype(np.float32)
    return jnp.asarray(bias.reshape(3, SWA_KV_HEADS, SWA_BAND_KEYS, SWA_GROUP * SWA_K_BLOCK))


def _swa_sink_lanes(b_sink):
    s = (b_sink.astype(F32) * LOG2E).reshape(SWA_KV_HEADS, 1, SWA_GROUP, 1)
    return jnp.broadcast_to(s, (SWA_KV_HEADS, 1, SWA_GROUP, SWA_K_BLOCK)).reshape(SWA_KV_HEADS, 1, -1)


def _trunk(x, mem, wa, wb, wmem, wots, sink, bias):
    ts = min(SEQ_TILE, x.shape[1])
    b = x.shape[0]
    nt = x.shape[1] // ts

    km, vmt = _memkv(mem, **wmem[0])
    qt, k, vt, qmt, sgt = _proj_a(x, wa, ts)
    mixt = _mla_attn(qt, k, vt)
    mix_spec = pl.BlockSpec((1, MLA_HEADS, 1, MLA_V, ts), lambda i, j: (i, 0, j, 0, 0))
    x = _post(x, mixt, mix_spec, qmt, sgt, km, vmt, wots[0], ts)

    km, vmt = _memkv(mem, **wmem[1])
    qt, k, vt, qmt, sgt = _proj_b(x, wb, min(PROJ_B_TILE, x.shape[1]))
    mixt = _swa_attn(qt, k, vt, bias, sink)
    mix_spec = pl.BlockSpec((1, MIX_WIDTH, ts), lambda i, j: (i, 0, j))
    return _post(x, mixt, mix_spec, qmt, sgt, km, vmt, wots[1], ts)


def kernel(x_prompt, x_sample, mem_prompt, mem_sample, norm_in, w_out, mem_norm, w_mem_kv, mem_q_norm, mem_k_norm, a_w_in, a_q_a_norm, a_w_q_b, a_kv_a_norm, a_w_kv_b, a_q_norm, a_k_norm, b_w_in, b_q_norm, b_k_norm, b_sink):
    assert norm_in.shape[0] == 2 and a_w_in.shape[0] == 1 and b_w_in.shape[0] == 1
    assert x_prompt.shape[1] == x_sample.shape[1]
    seq = x_prompt.shape[1]
    assert seq % min(SWA_Q_TILE, seq) == 0 and seq % min(SEQ_TILE, seq) == 0
    wa = _prep_layer_a(seq, norm_in[0], a_w_in[0], a_q_a_norm[0], a_w_q_b[0], a_kv_a_norm[0], a_w_kv_b[0],
                       a_q_norm[0], a_k_norm[0])
    wb = _prep_layer_b(norm_in[1], b_w_in[0], b_q_norm[0], b_k_norm[0])
    wmem = [_prep_mem(mem_norm[i], w_mem_kv[i], mem_k_norm[i], mem_q_norm[i]) for i in range(2)]
    wots = [w_out[i].astype(BF16) for i in range(2)]
    sink = _swa_sink_lanes(b_sink[0])
    bias = _swa_bias()
    y_prompt = _trunk(x_prompt, mem_prompt, wa, wb, wmem, wots, sink, bias)
    y_sample = _trunk(x_sample, mem_sample, wa, wb, wmem, wots, sink, bias)
    return (y_prompt, y_sample)
```

```python
import math

import jax
import jax.numpy as jnp
import numpy as np
from jax import lax
from jax.experimental import pallas as pl
from jax.experimental.pallas import tpu as pltpu

D_MODEL = 1024
N_MEM = 256
MEM_HEADS = 4
MEM_HEAD_DIM = 128
MEM_WIDTH = MEM_HEADS * MEM_HEAD_DIM

MLA_HEADS = 8
MLA_Q_RANK = 384
MLA_KV_RANK = 256
MLA_NOPE = 128
MLA_ROPE = 64
MLA_HALF = MLA_ROPE // 2
MLA_V = 128
MLA_QK = MLA_NOPE + MLA_ROPE
ROPE_THETA = 10000.0

SWA_HEADS = 16
SWA_KV_HEADS = 2
SWA_HEAD_DIM = 64
SWA_GROUP = SWA_HEADS // SWA_KV_HEADS
SWA_Q_W = SWA_HEADS * SWA_HEAD_DIM
SWA_KV_W = SWA_KV_HEADS * SWA_HEAD_DIM
WINDOW = 128

MIX_WIDTH = 1024
BRANCH_WIDTH = MIX_WIDTH + MEM_WIDTH
EPS = 1e-6
NEG = -1e30

V7X_LANES = 128
V7X_VMEM_BYTES = 64 * 1024 * 1024

SEQ_TILE = 512
PROJ_A_SUB_TILE = 512
PROJ_B_TILE = 1024
PROJ_B_SUB_TILE = 256
MLA_SCORE_SLOTS = 8
MLA_HEADS_PER_STEP = 2
SWA_Q_TILE = 1024
SWA_K_BLOCK = 128
SWA_WINDOW_BLOCKS = 2 * WINDOW // SWA_K_BLOCK
SWA_BAND_KEYS = SWA_K_BLOCK + 2 * WINDOW
SWA_HEADS_PER_CHAIN = 2
SWA_CHAIN_LANES = SWA_HEADS_PER_CHAIN * SWA_K_BLOCK
SWA_SCORE_SLOTS = 4
LOG2E = math.log2(math.e)
MLA_Q_SCALE = MLA_QK ** -0.5 * LOG2E
SWA_Q_SCALE = SWA_HEAD_DIM ** -0.5 * LOG2E
MEM_Q_SCALE = MEM_HEAD_DIM ** -0.5 * LOG2E
ONES_ROWS = 16

F32 = jnp.float32
BF16 = jnp.bfloat16

_NT = (((1,), (1,)), ((), ()))
_TN = (((0,), (0,)), ((), ()))


def _vmem_limit(nbytes):
    return int(min(nbytes, V7X_VMEM_BYTES - 8 * 1024 * 1024))


def _params(nbytes, ndims):
    return pltpu.CompilerParams(
        dimension_semantics=("arbitrary",) * ndims,
        vmem_limit_bytes=_vmem_limit(nbytes))


def _dot(a, b):
    return jnp.dot(a, b, preferred_element_type=F32)


def _dot_nt(a, b):
    return lax.dot_general(a, b, _NT, preferred_element_type=F32)


def _dot_tn(a, b):
    return lax.dot_general(a, b, _TN, preferred_element_type=F32)


def _rms_lanes(x):
    return x * lax.rsqrt(jnp.mean(x * x, axis=-1, keepdims=True) + EPS)


def _rms_scale_rows(x):
    return lax.rsqrt(jnp.mean(x * x, axis=0, keepdims=True) + EPS)


def _full(shape):
    return pl.BlockSpec(shape, lambda *_: (0,) * len(shape))


def _memkv_kernel(mem_ref, wk_ref, wvt_ref, gk_ref, km_ref, vmt_ref):
    mn = _rms_lanes(mem_ref[0]).astype(BF16)
    k = _dot(mn, wk_ref[...])
    vt = _dot_nt(wvt_ref[...], mn)
    for h in range(MEM_HEADS):
        lo = h * MEM_HEAD_DIM
        kh = k[:, lo:lo + MEM_HEAD_DIM]
        km_ref[0, h] = (_rms_lanes(kh) * gk_ref[...]).astype(BF16)
        vmt_ref[0, h, :MEM_HEAD_DIM, :] = vt[lo:lo + MEM_HEAD_DIM].astype(BF16)
        vmt_ref[0, h, MEM_HEAD_DIM:, :] = jnp.ones((ONES_ROWS, N_MEM), BF16)


def _memkv(mem, wk, wvt, gk):
    b = mem.shape[0]
    return pl.pallas_call(
        _memkv_kernel,
        grid=(b,),
        in_specs=[
            pl.BlockSpec((1, N_MEM, D_MODEL), lambda i: (i, 0, 0)),
            _full((D_MODEL, MEM_WIDTH)),
            _full((MEM_WIDTH, D_MODEL)),
            _full((1, MEM_HEAD_DIM)),
        ],
        out_specs=[
            pl.BlockSpec((1, MEM_HEADS, N_MEM, MEM_HEAD_DIM), lambda i: (i, 0, 0, 0)),
            pl.BlockSpec((1, MEM_HEADS, MEM_HEAD_DIM + ONES_ROWS, N_MEM), lambda i: (i, 0, 0, 0)),
        ],
        out_shape=[
            jax.ShapeDtypeStruct((b, MEM_HEADS, N_MEM, MEM_HEAD_DIM), BF16),
            jax.ShapeDtypeStruct((b, MEM_HEADS, MEM_HEAD_DIM + ONES_ROWS, N_MEM), BF16),
        ],
        compiler_params=_params(24 * 2**20, 1),
        name="memkv",
    )(mem, wk, wvt, gk)


def _normed_input(x_ref, tok):
    return _rms_lanes(x_ref[0, tok, :]).astype(BF16)


def _mem_query_and_gate(zt, qmt_ref, sgt_ref, tok):
    for h in range(MEM_HEADS):
        lo = h * MEM_HEAD_DIM
        qm = zt[lo:lo + MEM_HEAD_DIM]
        qmt_ref[0, lo:lo + MEM_HEAD_DIM, tok] = (qm * (_rms_scale_rows(qm) * MEM_Q_SCALE)).astype(BF16)
    hg = 0.5 * zt[MEM_WIDTH:]
    sgt_ref[0, :, tok] = (hg + hg * jnp.tanh(hg)).astype(BF16)


def _sub_tiles(ts, sub):
    sub = min(sub, ts)
    return [slice(c * sub, (c + 1) * sub) for c in range(ts // sub)]


def _proj_a_kernel(x_ref, w1_ref, wgt_ref, wqbt_ref, wkn_ref, wvt_ref,
                   qtab_ref, gkn_ref, gkr_ref, ctab_ref, stab_ref,
                   qt_ref, k_ref, vt_ref, qmt_ref, sgt_ref):
    kv_lo = MLA_Q_RANK
    kr_lo = MLA_Q_RANK + MLA_KV_RANK
    for tok in _sub_tiles(x_ref.shape[1], PROJ_A_SUB_TILE):
        h = _normed_input(x_ref, tok)
        z1 = _dot(h, w1_ref[...])
        _mem_query_and_gate(_dot_nt(wgt_ref[...], h), qmt_ref, sgt_ref, tok)
        cqn = _rms_lanes(z1[:, :kv_lo]).astype(BF16)
        ckvn = _rms_lanes(z1[:, kv_lo:kr_lo]).astype(BF16)
        kr = z1[:, kr_lo:]

        qt = _dot_nt(wqbt_ref[...], cqn)
        a1 = qtab_ref[0 * MLA_HALF:1 * MLA_HALF, tok]
        b1 = qtab_ref[1 * MLA_HALF:2 * MLA_HALF, tok]
        a2 = qtab_ref[2 * MLA_HALF:3 * MLA_HALF, tok]
        b2 = qtab_ref[3 * MLA_HALF:4 * MLA_HALF, tok]
        for hh in range(MLA_HEADS):
            qh = qt[hh * MLA_QK:(hh + 1) * MLA_QK]
            r = _rms_scale_rows(qh)
            x1 = qh[MLA_NOPE:MLA_NOPE + MLA_HALF]
            x2 = qh[MLA_NOPE + MLA_HALF:]
            qt_ref[0, hh, 0, :MLA_NOPE, tok] = (qh[:MLA_NOPE] * (r * MLA_Q_SCALE)).astype(BF16)
            qt_ref[0, hh, 0, MLA_NOPE:MLA_NOPE + MLA_HALF, tok] = ((x1 * a1 - x2 * b1) * r).astype(BF16)
            qt_ref[0, hh, 0, MLA_NOPE + MLA_HALF:, tok] = ((x2 * a2 + x1 * b2) * r).astype(BF16)

        kn = _dot(ckvn, wkn_ref[...])
        vt = _dot_nt(wvt_ref[...], ckvn)
        krg = kr * gkr_ref[...]
        kro = krg * ctab_ref[tok, :] + pltpu.roll(krg, MLA_HALF, 1) * stab_ref[tok, :]
        ss_r = 0.5 * jnp.sum(kr * kr, axis=1, keepdims=True)
        for hh in range(MLA_HEADS):
            knh = kn[:, hh * MLA_NOPE:(hh + 1) * MLA_NOPE]
            ss = jnp.sum(knh * knh, axis=1, keepdims=True) + ss_r
            r = lax.rsqrt(ss * (1.0 / MLA_QK) + EPS)
            k_ref[0, hh, tok, :MLA_NOPE] = (knh * gkn_ref[...] * r).astype(BF16)
            k_ref[0, hh, tok, MLA_NOPE:] = (kro[:, :MLA_ROPE] * r).astype(BF16)
            vt_ref[0, hh, 0, :MLA_V, tok] = vt[hh * MLA_V:(hh + 1) * MLA_V].astype(BF16)
            vt_ref[0, hh, 0, MLA_V:, tok] = jnp.ones((ONES_ROWS, vt.shape[1]), BF16)


def _proj_a(x, wa, ts):
    b, s, _ = x.shape
    nt = s // ts
    consts = [wa["w1"], wa["wgt"], wa["wqbt"], wa["wkn"], wa["wvt"]]
    tail = [wa["gkn"], wa["gkr"]]
    in_specs = (
        [pl.BlockSpec((1, ts, D_MODEL), lambda i, j: (i, j, 0))]
        + [_full(c.shape) for c in consts]
        + [pl.BlockSpec((4 * MLA_HALF, ts), lambda i, j: (0, j))]
        + [_full(c.shape) for c in tail]
        + [pl.BlockSpec((ts, V7X_LANES), lambda i, j: (j, 0))] * 2
    )
    out_shape = [
        jax.ShapeDtypeStruct((b, MLA_HEADS, nt, MLA_QK, ts), BF16),
        jax.ShapeDtypeStruct((b, MLA_HEADS, s, MLA_QK), BF16),
        jax.ShapeDtypeStruct((b, MLA_HEADS, nt, MLA_V + ONES_ROWS, ts), BF16),
        jax.ShapeDtypeStruct((b, MEM_WIDTH, s), BF16),
        jax.ShapeDtypeStruct((b, BRANCH_WIDTH, s), BF16),
    ]
    out_specs = [
        pl.BlockSpec((1, MLA_HEADS, 1, MLA_QK, ts), lambda i, j: (i, 0, j, 0, 0)),
        pl.BlockSpec((1, MLA_HEADS, ts, MLA_QK), lambda i, j: (i, 0, j, 0)),
        pl.BlockSpec((1, MLA_HEADS, 1, MLA_V + ONES_ROWS, ts), lambda i, j: (i, 0, j, 0, 0)),
        pl.BlockSpec((1, MEM_WIDTH, ts), lambda i, j: (i, 0, j)),
        pl.BlockSpec((1, BRANCH_WIDTH, ts), lambda i, j: (i, 0, j)),
    ]
    return pl.pallas_call(
        _proj_a_kernel,
        grid=(b, nt),
        in_specs=in_specs,
        out_specs=out_specs,
        out_shape=out_shape,
        compiler_params=_params(56 * 2**20, 2),
        name="proj_a",
    )(x, *consts, wa["qtab"], *tail, wa["ctab"], wa["stab"])


def _mla_attn_kernel(qt_ref, k_ref, vt_ref, o_ref, s_sc, mx_sc, acc_sc):
    nq = qt_ref.shape[2]
    nk = vt_ref.shape[2]
    tk = vt_ref.shape[4]
    tq = qt_ref.shape[4]
    nslots = s_sc.shape[0]
    ahead = nslots - 1

    n_items = qt_ref.shape[1] * nq

    def put_scores(item, kj):
        hh, qi = item // nq, item % nq
        st = _dot(k_ref[0, hh, kj * tk:(kj + 1) * tk, :], qt_ref[0, hh, qi])
        s_sc[kj % nslots] = st
        mx_sc[kj % nslots] = jnp.max(st, axis=0, keepdims=True)

    for kj in range(ahead):
        put_scores(0, kj)

    def q_tile(item, is_last):
        hh, qi = item // nq, item % nq
        known = min(ahead, nk)
        m = jnp.full((1, tq), NEG, F32)
        for kj in range(known):
            m = jnp.maximum(m, mx_sc[kj % nslots])
        for kj in range(nk):
            cur = kj % nslots
            if kj + ahead < nk:
                put_scores(item, kj + ahead)
            elif not is_last:
                put_scores(item + 1, kj + ahead - nk)
            m_new = m if kj < known else jnp.maximum(m, mx_sc[cur])
            p = jnp.exp2((s_sc[cur] - m_new).astype(BF16))
            pv = _dot(vt_ref[0, hh, kj], p)
            if kj == 0:
                acc_sc[...] = pv
            elif kj < known:
                acc_sc[...] += pv
            else:
                acc_sc[...] = jnp.exp2(m - m_new) * acc_sc[...] + pv
            m = m_new
        acc = acc_sc[...]
        o_ref[0, hh, qi] = (acc[:MLA_V] * (1.0 / acc[MLA_V:MLA_V + 1])).astype(BF16)

    def q_tile_with_lookahead(item, carry):
        q_tile(item, is_last=False)
        return carry

    lax.fori_loop(0, n_items - 1, q_tile_with_lookahead, 0)
    q_tile(n_items - 1, is_last=True)


def _mla_attn(qt, k, vt):
    b, nh, nt, _, ts = qt.shape
    s = k.shape[2]
    nslots = min(MLA_SCORE_SLOTS, nt)
    assert nt % nslots == 0
    hps = MLA_HEADS_PER_STEP
    return pl.pallas_call(
        _mla_attn_kernel,
        grid=(b, nh // hps),
        in_specs=[
            pl.BlockSpec((1, hps, nt, MLA_QK, ts), lambda i, j: (i, j, 0, 0, 0)),
            pl.BlockSpec((1, hps, s, MLA_QK), lambda i, j: (i, j, 0, 0)),
            pl.BlockSpec((1, hps, nt, MLA_V + ONES_ROWS, ts), lambda i, j: (i, j, 0, 0, 0)),
        ],
        out_specs=pl.BlockSpec((1, hps, nt, MLA_V, ts), lambda i, j: (i, j, 0, 0, 0)),
        out_shape=jax.ShapeDtypeStruct((b, nh, nt, MLA_V, ts), BF16),
        scratch_shapes=[
            pltpu.VMEM((nslots, ts, ts), F32),
            pltpu.VMEM((nslots, 1, ts), F32),
            pltpu.VMEM((MLA_V + ONES_ROWS, ts), F32),
        ],
        compiler_params=_params(52 * 2**20, 2),
        name="mla_attn",
    )(qt, k, vt)


def _post_kernel(x_ref, mixt_ref, qmt_ref, sgt_ref, km_ref, vmt_ref, wo_ref, o_ref, s_sc):
    ts = x_ref.shape[1]
    for h in range(MEM_HEADS):
        lo = h * MEM_HEAD_DIM
        s_sc[h] = _dot(km_ref[0, h], qmt_ref[0, lo:lo + MEM_HEAD_DIM, :])
    sg = sgt_ref[0]
    bmix = mixt_ref[...].reshape(MIX_WIDTH, ts) * sg[:MIX_WIDTH]
    out = _dot_tn(bmix, wo_ref[:MIX_WIDTH])
    memo = []
    for h in range(MEM_HEADS):
        st = s_sc[h]
        p = jnp.exp2((st - jnp.max(st, axis=0, keepdims=True)).astype(BF16))
        pv = _dot(vmt_ref[0, h], p)
        memo.append(pv[:MEM_HEAD_DIM] * (1.0 / pv[MEM_HEAD_DIM:MEM_HEAD_DIM + 1]))
    bmem = (jnp.concatenate(memo, axis=0) * sg[MIX_WIDTH:].astype(F32)).astype(BF16)
    o_ref[0] = x_ref[0] + out + _dot_tn(bmem, wo_ref[MIX_WIDTH:])


def _post(x, mixt, mixt_spec, qmt, sgt, km, vmt, wot, ts):
    b, s, _ = x.shape
    return pl.pallas_call(
        _post_kernel,
        grid=(b, s // ts),
        in_specs=[
            pl.BlockSpec((1, ts, D_MODEL), lambda i, j: (i, j, 0)),
            mixt_spec,
            pl.BlockSpec((1, MEM_WIDTH, ts), lambda i, j: (i, 0, j)),
            pl.BlockSpec((1, BRANCH_WIDTH, ts), lambda i, j: (i, 0, j)),
            pl.BlockSpec((1, MEM_HEADS, N_MEM, MEM_HEAD_DIM), lambda i, j: (i, 0, 0, 0)),
            pl.BlockSpec((1, MEM_HEADS, MEM_HEAD_DIM + ONES_ROWS, N_MEM), lambda i, j: (i, 0, 0, 0)),
            _full(wot.shape),
        ],
        out_specs=pl.BlockSpec((1, ts, D_MODEL), lambda i, j: (i, j, 0)),
        out_shape=jax.ShapeDtypeStruct(x.shape, x.dtype),
        scratch_shapes=[pltpu.VMEM((MEM_HEADS, N_MEM, ts), F32)],
        compiler_params=_params(48 * 2**20, 2),
        name="post",
    )(x, mixt, qmt, sgt, km, vmt, wot)


def _proj_b_kernel(x_ref, wbt_ref, gk_ref, qt_ref, k_ref, vt_ref, qmt_ref, sgt_ref):
    for tok in _sub_tiles(x_ref.shape[1], PROJ_B_SUB_TILE):
        h = _normed_input(x_ref, tok)
        _mem_query_and_gate(_dot_nt(wbt_ref[SWA_Q_W + 2 * SWA_KV_W:], h), qmt_ref, sgt_ref, tok)
        qt = _dot_nt(wbt_ref[:SWA_Q_W], h)
        for hh in range(SWA_HEADS):
            lo = hh * SWA_HEAD_DIM
            qh = qt[lo:lo + SWA_HEAD_DIM]
            qn = (qh * (_rms_scale_rows(qh) * SWA_Q_SCALE)).astype(BF16)
            g, hq = divmod(hh, SWA_GROUP)
            for c in range(qn.shape[1] // SWA_K_BLOCK):
                qt_ref[0, g, tok.start // SWA_K_BLOCK + c, :, hq * SWA_K_BLOCK:(hq + 1) * SWA_K_BLOCK] = (
                    qn[:, c * SWA_K_BLOCK:(c + 1) * SWA_K_BLOCK])
        kvt = _dot_nt(wbt_ref[SWA_Q_W:SWA_Q_W + 2 * SWA_KV_W], h)
        kparts = []
        for g in range(SWA_KV_HEADS):
            kh = kvt[g * SWA_HEAD_DIM:(g + 1) * SWA_HEAD_DIM]
            kparts.append(kh * _rms_scale_rows(kh))
        k_ref[0, tok, :] = (jnp.concatenate(kparts, axis=0).T * gk_ref[...]).astype(BF16)
        vt = kvt[SWA_KV_W:].astype(BF16)
        for c in range(vt.shape[1] // SWA_K_BLOCK):
            vt_ref[0, tok.start // SWA_K_BLOCK + c] = vt[:, c * SWA_K_BLOCK:(c + 1) * SWA_K_BLOCK]


def _proj_b(x, wb, ts):
    b, s, _ = x.shape
    consts = [wb["wbt"], wb["gk"]]
    return pl.pallas_call(
        _proj_b_kernel,
        grid=(b, s // ts),
        in_specs=[pl.BlockSpec((1, ts, D_MODEL), lambda i, j: (i, j, 0))] + [_full(c.shape) for c in consts],
        out_specs=[
            pl.BlockSpec((1, SWA_KV_HEADS, ts // SWA_K_BLOCK, SWA_HEAD_DIM, SWA_GROUP * SWA_K_BLOCK),
                         lambda i, j: (i, 0, j, 0, 0)),
            pl.BlockSpec((1, ts, SWA_KV_W), lambda i, j: (i, j, 0)),
            pl.BlockSpec((1, ts // SWA_K_BLOCK, SWA_KV_W, SWA_K_BLOCK), lambda i, j: (i, j, 0, 0)),
            pl.BlockSpec((1, MEM_WIDTH, ts), lambda i, j: (i, 0, j)),
            pl.BlockSpec((1, BRANCH_WIDTH, ts), lambda i, j: (i, 0, j)),
        ],
        out_shape=[
            jax.ShapeDtypeStruct((b, SWA_KV_HEADS, s // SWA_K_BLOCK, SWA_HEAD_DIM, SWA_GROUP * SWA_K_BLOCK), BF16),
            jax.ShapeDtypeStruct((b, s, SWA_KV_W), BF16),
            jax.ShapeDtypeStruct((b, s // SWA_K_BLOCK, SWA_KV_W, SWA_K_BLOCK), BF16),
            jax.ShapeDtypeStruct((b, MEM_WIDTH, s), BF16),
            jax.ShapeDtypeStruct((b, BRANCH_WIDTH, s), BF16),
        ],
        compiler_params=_params(56 * 2**20, 2),
        name="proj_b",
    )(x, *consts)


def _alibi_slope(h):
    return 2.0 ** (-8.0 * (h + 1) / SWA_HEADS)


def _swa_kernel(qt_ref, k_ref, vt_ref, bias_lo_ref, bias_mid_ref, bias_hi_ref, sink_ref, o_ref, s_sc, mx_sc):
    per_tile = qt_ref.shape[2]
    n_blocks = vt_ref.shape[1]
    first = pl.program_id(1) * per_tile - SWA_WINDOW_BLOCKS // 2
    blocks = [jnp.clip(first + o, 0, n_blocks - 1) for o in range(per_tile + SWA_WINDOW_BLOCKS)]
    k_all = jnp.concatenate(
        [k_ref[0, pl.ds(pl.multiple_of(blk * SWA_K_BLOCK, SWA_K_BLOCK), SWA_K_BLOCK), :] for blk in blocks],
        axis=0)
    vt_all = jnp.concatenate([vt_ref[0, blk] for blk in blocks], axis=1)
    bias_refs = (bias_lo_ref,) + (bias_mid_ref,) * (per_tile - 2) + (bias_hi_ref,)
    chains = [(sb, g, c) for sb in range(per_tile) for g in range(SWA_KV_HEADS)
              for c in range(SWA_GROUP // SWA_HEADS_PER_CHAIN)]
    k_g, vt_g = {}, {}
    for sb in range(len(bias_refs)):
        k0 = sb * SWA_K_BLOCK
        for g in range(SWA_KV_HEADS):
            k_g[sb, g] = k_all[k0:k0 + SWA_BAND_KEYS, g * SWA_HEAD_DIM:(g + 1) * SWA_HEAD_DIM]
            vt_g[sb, g] = jnp.concatenate(
                [vt_all[g * SWA_HEAD_DIM:(g + 1) * SWA_HEAD_DIM, k0:k0 + SWA_BAND_KEYS],
                 jnp.ones((ONES_ROWS, SWA_BAND_KEYS), BF16)], axis=0)

    def put_scores(slot, chain):
        sb, g, c = chain
        lanes = slice(c * SWA_CHAIN_LANES, (c + 1) * SWA_CHAIN_LANES)
        st = _dot(k_g[sb, g], qt_ref[0, g, sb, :, lanes]) + bias_refs[sb][0, g, :, lanes]
        s_sc[slot] = st
        mx_sc[slot] = jnp.max(st, axis=0, keepdims=True)

    nslots = s_sc.shape[0]
    ahead = nslots - 1
    for t in range(min(ahead, len(chains))):
        put_scores(t, chains[t])
    for t, (sb, g, c) in enumerate(chains):
        if t + ahead < len(chains):
            put_scores((t + ahead) % nslots, chains[t + ahead])
        sink = sink_ref[g, :, c * SWA_CHAIN_LANES:(c + 1) * SWA_CHAIN_LANES]
        m = jnp.maximum(mx_sc[t % nslots], sink)
        p = jnp.exp2((s_sc[t % nslots] - m).astype(BF16))
        pv = _dot(vt_g[sb, g], p)
        l = pv[SWA_HEAD_DIM:SWA_HEAD_DIM + 1] + jnp.exp2(sink - m)
        ot = pv[:SWA_HEAD_DIM] * (1.0 / l)
        for hh in range(SWA_HEADS_PER_CHAIN):
            lo = ((g * SWA_GROUP) + c * SWA_HEADS_PER_CHAIN + hh) * SWA_HEAD_DIM
            o_ref[0, lo:lo + SWA_HEAD_DIM, sb * SWA_K_BLOCK:(sb + 1) * SWA_K_BLOCK] = (
                ot[:, hh * SWA_K_BLOCK:(hh + 1) * SWA_K_BLOCK].astype(BF16))


def _swa_attn(qt, k, vt, bias, sink):
    b, s = k.shape[0], k.shape[1]
    tq = min(SWA_Q_TILE, s)
    per_tile = tq // SWA_K_BLOCK
    assert per_tile >= 2
    last_tile = s // tq - 1
    bias_block = (1,) + bias.shape[1:]
    return pl.pallas_call(
        _swa_kernel,
        grid=(b, s // tq),
        in_specs=(
            [pl.BlockSpec((1, SWA_KV_HEADS, per_tile, SWA_HEAD_DIM, SWA_GROUP * SWA_K_BLOCK),
                          lambda i, j: (i, 0, j, 0, 0))]
            + [pl.BlockSpec((1,) + k.shape[1:], lambda i, j: (i, 0, 0)),
               pl.BlockSpec((1,) + vt.shape[1:], lambda i, j: (i, 0, 0, 0))]
            + [pl.BlockSpec(bias_block, lambda i, j: (jnp.where(j == 0, 0, 1), 0, 0, 0)),
               pl.BlockSpec(bias_block, lambda i, j: (1, 0, 0, 0)),
               pl.BlockSpec(bias_block, lambda i, j: (jnp.where(j == last_tile, 2, 1), 0, 0, 0)),
               _full(sink.shape)]
        ),
        out_specs=pl.BlockSpec((1, SWA_Q_W, tq), lambda i, j: (i, 0, j)),
        out_shape=jax.ShapeDtypeStruct((b, SWA_Q_W, s), BF16),
        scratch_shapes=[pltpu.VMEM((SWA_SCORE_SLOTS, SWA_BAND_KEYS, SWA_CHAIN_LANES), F32),
                        pltpu.VMEM((SWA_SCORE_SLOTS, 1, SWA_CHAIN_LANES), F32)],
        compiler_params=_params(48 * 2**20, 2),
        name="swa_attn",
    )(qt, k, vt, bias, bias, bias, sink)


def _col(v):
    return v.astype(F32)[:, None]


def _row(v):
    return v.astype(F32)[None, :]


def _prep_layer_a(seq, norm_in, a_w_in, a_q_a_norm, a_w_q_b, a_kv_a_norm, a_w_kv_b, a_q_norm, a_k_norm):
    kr_lo = MLA_Q_RANK + MLA_KV_RANK
    qm_lo = kr_lo + MLA_ROPE
    w_in = _col(norm_in) * a_w_in
    kr_cols = w_in[:, kr_lo:qm_lo]
    wkv = (_col(a_kv_a_norm) * a_w_kv_b).reshape(MLA_KV_RANK, MLA_HEADS, MLA_NOPE + MLA_V)
    inv = 1.0 / (ROPE_THETA ** (np.arange(0, MLA_ROPE, 2, dtype=np.float64) / MLA_ROPE))
    ang = np.arange(seq, dtype=np.float64)[:, None] * inv[None, :]
    cos, sin = np.cos(ang).astype(np.float32), np.sin(ang).astype(np.float32)
    g1 = _col(a_q_norm[MLA_NOPE:MLA_NOPE + MLA_HALF]) * MLA_Q_SCALE
    g2 = _col(a_q_norm[MLA_NOPE + MLA_HALF:]) * MLA_Q_SCALE
    return {
        "w1": jnp.concatenate([w_in[:, :kr_lo], kr_cols, kr_cols], axis=1).astype(BF16),
        "wgt": w_in[:, qm_lo:].T.astype(BF16),
        "wqbt": (_col(a_q_a_norm) * a_w_q_b).T.astype(BF16),
        "wkn": wkv[:, :, :MLA_NOPE].reshape(MLA_KV_RANK, MLA_HEADS * MLA_NOPE).astype(BF16),
        "wvt": wkv[:, :, MLA_NOPE:].reshape(MLA_KV_RANK, MLA_HEADS * MLA_V).T.astype(BF16),
        "qtab": jnp.concatenate([g1 * cos.T, g2 * sin.T, g2 * cos.T, g1 * sin.T], axis=0),
        "gkn": _row(a_k_norm[:MLA_NOPE] * a_q_norm[:MLA_NOPE]),
        "gkr": _row(jnp.tile(a_k_norm[MLA_NOPE:], 2)),
        "ctab": jnp.asarray(np.tile(cos, (1, 4))),
        "stab": jnp.asarray(np.concatenate([-sin, sin, -sin, sin], axis=1)),
    }


def _prep_layer_b(norm_in, b_w_in, b_q_norm, b_k_norm):
    return {
        "wbt": (_col(norm_in) * b_w_in).T.astype(BF16),
        "gk": _row(jnp.tile(b_k_norm * b_q_norm, SWA_KV_HEADS)),
    }


def _prep_mem(mem_norm, w_mem_kv, mem_k_norm, mem_q_norm):
    w = _col(mem_norm) * w_mem_kv
    return {
        "wk": w[:, :MEM_WIDTH].astype(BF16),
        "wvt": w[:, MEM_WIDTH:].T.astype(BF16),
        "gk": _row(mem_k_norm * mem_q_norm),
    }


def _swa_bias():
    r = np.arange(SWA_BAND_KEYS)[:, None]
    c = np.arange(SWA_K_BLOCK)[None, :]
    rel = r - WINDOW - c
    dist = np.abs(rel).astype(np.float64)
    slopes = np.array([_alibi_slope(h) for h in range(SWA_HEADS)]).reshape(SWA_KV_HEADS, SWA_GROUP)
    alibi = -(LOG2E * slopes)[:, None, :, None] * dist[None, :, None, :]
    in_window = (np.abs(rel) <= WINDOW)[None, None, :, None, :]
    key_block = (r // SWA_K_BLOCK)[None, None, :, None, :]
    variant = np.arange(3)[:, None, None, None, None]
    in_seq = ~(((variant == 0) & (key_block == 0)) | ((variant == 2) & (key_block == 2)))
    bias = np.where(in_window & in_seq, alibi[None], NEG).astype(np.float32)
    return jnp.asarray(bias.reshape(3, SWA_KV_HEADS, SWA_BAND_KEYS, SWA_GROUP * SWA_K_BLOCK))


def _swa_sink_lanes(b_sink):
    s = (b_sink.astype(F32) * LOG2E).reshape(SWA_KV_HEADS, 1, SWA_GROUP, 1)
    return jnp.broadcast_to(s, (SWA_KV_HEADS, 1, SWA_GROUP, SWA_K_BLOCK)).reshape(SWA_KV_HEADS, 1, -1)


def _trunk(x, mem, wa, wb, wmem, wots, sink, bias):
    ts = min(SEQ_TILE, x.shape[1])
    b = x.shape[0]
    nt = x.shape[1] // ts

    km, vmt = _memkv(mem, **wmem[0])
    qt, k, vt, qmt, sgt = _proj_a(x, wa, ts)
    mixt = _mla_attn(qt, k, vt)
    mix_spec = pl.BlockSpec((1, MLA_HEADS, 1, MLA_V, ts), lambda i, j: (i, 0, j, 0, 0))
    x = _post(x, mixt, mix_spec, qmt, sgt, km, vmt, wots[0], ts)

    km, vmt = _memkv(mem, **wmem[1])
    qt, k, vt, qmt, sgt = _proj_b(x, wb, min(PROJ_B_TILE, x.shape[1]))
    mixt = _swa_attn(qt, k, vt, bias, sink)
    mix_spec = pl.BlockSpec((1, MIX_WIDTH, ts), lambda i, j: (i, 0, j))
    return _post(x, mixt, mix_spec, qmt, sgt, km, vmt, wots[1], ts)


def kernel(x_prompt, x_sample, mem_prompt, mem_sample, norm_in, w_out, mem_norm, w_mem_kv, mem_q_norm, mem_k_norm, a_w_in, a_q_a_norm, a_w_q_b, a_kv_a_norm, a_w_kv_b, a_q_norm, a_k_norm, b_w_in, b_q_norm, b_k_norm, b_sink):
    assert norm_in.shape[0] == 2 and a_w_in.shape[0] == 1 and b_w_in.shape[0] == 1
    assert x_prompt.shape[1] == x_sample.shape[1]
    seq = x_prompt.shape[1]
    assert seq % min(SWA_Q_TILE, seq) == 0 and seq % min(SEQ_TILE, seq) == 0
    wa = _prep_layer_a(seq, norm_in[0], a_w_in[0], a_q_a_norm[0], a_w_q_b[0], a_kv_a_norm[0], a_w_kv_b[0],
                       a_q_norm[0], a_k_norm[0])
    wb = _prep_layer_b(norm_in[1], b_w_in[0], b_q_norm[0], b_k_norm[0])
    wmem = [_prep_mem(mem_norm[i], w_mem_kv[i], mem_k_norm[i], mem_q_norm[i]) for i in range(2)]
    wots = [w_out[i].astype(BF16) for i in range(2)]
    sink = _swa_sink_lanes(b_sink[0])
    bias = _swa_bias()
    y_prompt = _trunk(x_prompt, mem_prompt, wa, wb, wmem, wots, sink, bias)
    y_sample = _trunk(x_sample, mem_sample, wa, wb, wmem, wots, sink, bias)
    return (y_prompt, y_sample)
```

```python
import math

import jax
import jax.numpy as jnp
import numpy as np
from jax import lax
from jax.experimental import pallas as pl
from jax.experimental.pallas import tpu as pltpu

D_MODEL = 1024
N_MEM = 256
MEM_HEADS = 4
MEM_HEAD_DIM = 128
MEM_WIDTH = MEM_HEADS * MEM_HEAD_DIM

MLA_HEADS = 8
MLA_Q_RANK = 384
MLA_KV_RANK = 256
MLA_NOPE = 128
MLA_ROPE = 64
MLA_HALF = MLA_ROPE // 2
MLA_V = 128
MLA_QK = MLA_NOPE + MLA_ROPE
ROPE_THETA = 10000.0

SWA_HEADS = 16
SWA_KV_HEADS = 2
SWA_HEAD_DIM = 64
SWA_GROUP = SWA_HEADS // SWA_KV_HEADS
SWA_Q_W = SWA_HEADS * SWA_HEAD_DIM
SWA_KV_W = SWA_KV_HEADS * SWA_HEAD_DIM
WINDOW = 128

MIX_WIDTH = 1024
BRANCH_WIDTH = MIX_WIDTH + MEM_WIDTH
EPS = 1e-6
NEG = -1e30

V7X_LANES = 128
V7X_VMEM_BYTES = 64 * 1024 * 1024

SEQ_TILE = 512
PROJ_A_SUB_TILE = 512
PROJ_B_TILE = 1024
PROJ_B_SUB_TILE = 256
MLA_SCORE_SLOTS = 8
MLA_HEADS_PER_STEP = 2
SWA_Q_TILE = 1024
SWA_K_BLOCK = 128
SWA_WINDOW_BLOCKS = 2 * WINDOW // SWA_K_BLOCK
SWA_BAND_KEYS = SWA_K_BLOCK + 2 * WINDOW
SWA_HEADS_PER_CHAIN = 2
SWA_CHAIN_LANES = SWA_HEADS_PER_CHAIN * SWA_K_BLOCK
SWA_SCORE_SLOTS = 5
LOG2E = math.log2(math.e)
MLA_Q_SCALE = MLA_QK ** -0.5 * LOG2E
SWA_Q_SCALE = SWA_HEAD_DIM ** -0.5 * LOG2E
MEM_Q_SCALE = MEM_HEAD_DIM ** -0.5 * LOG2E
ONES_ROWS = 16

F32 = jnp.float32
BF16 = jnp.bfloat16

_NT = (((1,), (1,)), ((), ()))
_TN = (((0,), (0,)), ((), ()))


def _vmem_limit(nbytes):
    return int(min(nbytes, V7X_VMEM_BYTES - 8 * 1024 * 1024))


def _params(nbytes, ndims):
    return pltpu.CompilerParams(
        dimension_semantics=("arbitrary",) * ndims,
        vmem_limit_bytes=_vmem_limit(nbytes))


def _dot(a, b):
    return jnp.dot(a, b, preferred_element_type=F32)


def _dot_nt(a, b):
    return lax.dot_general(a, b, _NT, preferred_element_type=F32)


def _dot_tn(a, b):
    return lax.dot_general(a, b, _TN, preferred_element_type=F32)


def _rms_lanes(x):
    return x * lax.rsqrt(jnp.mean(x * x, axis=-1, keepdims=True) + EPS)


def _rms_scale_rows(x):
    return lax.rsqrt(jnp.mean(x * x, axis=0, keepdims=True) + EPS)


def _full(shape):
    return pl.BlockSpec(shape, lambda *_: (0,) * len(shape))


def _memkv_kernel(mem_ref, wk_ref, wvt_ref, gk_ref, km_ref, vmt_ref):
    mn = _rms_lanes(mem_ref[0]).astype(BF16)
    k = _dot(mn, wk_ref[...])
    vt = _dot_nt(wvt_ref[...], mn)
    for h in range(MEM_HEADS):
        lo = h * MEM_HEAD_DIM
        kh = k[:, lo:lo + MEM_HEAD_DIM]
        km_ref[0, h] = (_rms_lanes(kh) * gk_ref[...]).astype(BF16)
        vmt_ref[0, h, :MEM_HEAD_DIM, :] = vt[lo:lo + MEM_HEAD_DIM].astype(BF16)
        vmt_ref[0, h, MEM_HEAD_DIM:, :] = jnp.ones((ONES_ROWS, N_MEM), BF16)


def _memkv(mem, wk, wvt, gk):
    b = mem.shape[0]
    return pl.pallas_call(
        _memkv_kernel,
        grid=(b,),
        in_specs=[
            pl.BlockSpec((1, N_MEM, D_MODEL), lambda i: (i, 0, 0)),
            _full((D_MODEL, MEM_WIDTH)),
            _full((MEM_WIDTH, D_MODEL)),
            _full((1, MEM_HEAD_DIM)),
        ],
        out_specs=[
            pl.BlockSpec((1, MEM_HEADS, N_MEM, MEM_HEAD_DIM), lambda i: (i, 0, 0, 0)),
            pl.BlockSpec((1, MEM_HEADS, MEM_HEAD_DIM + ONES_ROWS, N_MEM), lambda i: (i, 0, 0, 0)),
        ],
        out_shape=[
            jax.ShapeDtypeStruct((b, MEM_HEADS, N_MEM, MEM_HEAD_DIM), BF16),
            jax.ShapeDtypeStruct((b, MEM_HEADS, MEM_HEAD_DIM + ONES_ROWS, N_MEM), BF16),
        ],
        compiler_params=_params(24 * 2**20, 1),
        name="memkv",
    )(mem, wk, wvt, gk)


def _normed_input(x_ref, tok):
    return _rms_lanes(x_ref[0, tok, :]).astype(BF16)


def _mem_query_and_gate(zt, qmt_ref, sgt_ref, tok):
    for h in range(MEM_HEADS):
        lo = h * MEM_HEAD_DIM
        qm = zt[lo:lo + MEM_HEAD_DIM]
        qmt_ref[0, lo:lo + MEM_HEAD_DIM, tok] = (qm * (_rms_scale_rows(qm) * MEM_Q_SCALE)).astype(BF16)
    hg = 0.5 * zt[MEM_WIDTH:]
    sgt_ref[0, :, tok] = (hg + hg * jnp.tanh(hg)).astype(BF16)


def _sub_tiles(ts, sub):
    sub = min(sub, ts)
    return [slice(c * sub, (c + 1) * sub) for c in range(ts // sub)]


def _proj_a_kernel(x_ref, w1_ref, wgt_ref, wqbt_ref, wkn_ref, wvt_ref,
                   qtab_ref, gkn_ref, gkr_ref, ctab_ref, stab_ref,
                   qt_ref, k_ref, vt_ref, qmt_ref, sgt_ref):
    kv_lo = MLA_Q_RANK
    kr_lo = MLA_Q_RANK + MLA_KV_RANK
    for tok in _sub_tiles(x_ref.shape[1], PROJ_A_SUB_TILE):
        h = _normed_input(x_ref, tok)
        z1 = _dot(h, w1_ref[...])
        _mem_query_and_gate(_dot_nt(wgt_ref[...], h), qmt_ref, sgt_ref, tok)
        cqn = _rms_lanes(z1[:, :kv_lo]).astype(BF16)
        ckvn = _rms_lanes(z1[:, kv_lo:kr_lo]).astype(BF16)
        kr = z1[:, kr_lo:]

        qt = _dot_nt(wqbt_ref[...], cqn)
        a1 = qtab_ref[0 * MLA_HALF:1 * MLA_HALF, tok]
        b1 = qtab_ref[1 * MLA_HALF:2 * MLA_HALF, tok]
        a2 = qtab_ref[2 * MLA_HALF:3 * MLA_HALF, tok]
        b2 = qtab_ref[3 * MLA_HALF:4 * MLA_HALF, tok]
        for hh in range(MLA_HEADS):
            qh = qt[hh * MLA_QK:(hh + 1) * MLA_QK]
            r = _rms_scale_rows(qh)
            x1 = qh[MLA_NOPE:MLA_NOPE + MLA_HALF]
            x2 = qh[MLA_NOPE + MLA_HALF:]
            qt_ref[0, hh, 0, :MLA_NOPE, tok] = (qh[:MLA_NOPE] * (r * MLA_Q_SCALE)).astype(BF16)
            qt_ref[0, hh, 0, MLA_NOPE:MLA_NOPE + MLA_HALF, tok] = ((x1 * a1 - x2 * b1) * r).astype(BF16)
            qt_ref[0, hh, 0, MLA_NOPE + MLA_HALF:, tok] = ((x2 * a2 + x1 * b2) * r).astype(BF16)

        kn = _dot(ckvn, wkn_ref[...])
        vt = _dot_nt(wvt_ref[...], ckvn)
        krg = kr * gkr_ref[...]
        kro = krg * ctab_ref[tok, :] + pltpu.roll(krg, MLA_HALF, 1) * stab_ref[tok, :]
        ss_r = 0.5 * jnp.sum(kr * kr, axis=1, keepdims=True)
        for hh in range(MLA_HEADS):
            knh = kn[:, hh * MLA_NOPE:(hh + 1) * MLA_NOPE]
            ss = jnp.sum(knh * knh, axis=1, keepdims=True) + ss_r
            r = lax.rsqrt(ss * (1.0 / MLA_QK) + EPS)
            k_ref[0, hh, tok, :MLA_NOPE] = (knh * gkn_ref[...] * r).astype(BF16)
            k_ref[0, hh, tok, MLA_NOPE:] = (kro[:, :MLA_ROPE] * r).astype(BF16)
            vt_ref[0, hh, 0, :MLA_V, tok] = vt[hh * MLA_V:(hh + 1) * MLA_V].astype(BF16)
            vt_ref[0, hh, 0, MLA_V:, tok] = jnp.ones((ONES_ROWS, vt.shape[1]), BF16)


def _proj_a(x, wa, ts):
    b, s, _ = x.shape
    nt = s // ts
    consts = [wa["w1"], wa["wgt"], wa["wqbt"], wa["wkn"], wa["wvt"]]
    tail = [wa["gkn"], wa["gkr"]]
    in_specs = (
        [pl.BlockSpec((1, ts, D_MODEL), lambda i, j: (i, j, 0))]
        + [_full(c.shape) for c in consts]
        + [pl.BlockSpec((4 * MLA_HALF, ts), lambda i, j: (0, j))]
        + [_full(c.shape) for c in tail]
        + [pl.BlockSpec((ts, V7X_LANES), lambda i, j: (j, 0))] * 2
    )
    out_shape = [
        jax.ShapeDtypeStruct((b, MLA_HEADS, nt, MLA_QK, ts), BF16),
        jax.ShapeDtypeStruct((b, MLA_HEADS, s, MLA_QK), BF16),
        jax.ShapeDtypeStruct((b, MLA_HEADS, nt, MLA_V + ONES_ROWS, ts), BF16),
        jax.ShapeDtypeStruct((b, MEM_WIDTH, s), BF16),
        jax.ShapeDtypeStruct((b, BRANCH_WIDTH, s), BF16),
    ]
    out_specs = [
        pl.BlockSpec((1, MLA_HEADS, 1, MLA_QK, ts), lambda i, j: (i, 0, j, 0, 0)),
        pl.BlockSpec((1, MLA_HEADS, ts, MLA_QK), lambda i, j: (i, 0, j, 0)),
        pl.BlockSpec((1, MLA_HEADS, 1, MLA_V + ONES_ROWS, ts), lambda i, j: (i, 0, j, 0, 0)),
        pl.BlockSpec((1, MEM_WIDTH, ts), lambda i, j: (i, 0, j)),
        pl.BlockSpec((1, BRANCH_WIDTH, ts), lambda i, j: (i, 0, j)),
    ]
    return pl.pallas_call(
        _proj_a_kernel,
        grid=(b, nt),
        in_specs=in_specs,
        out_specs=out_specs,
        out_shape=out_shape,
        compiler_params=_params(56 * 2**20, 2),
        name="proj_a",
    )(x, *consts, wa["qtab"], *tail, wa["ctab"], wa["stab"])


def _mla_attn_kernel(qt_ref, k_ref, vt_ref, o_ref, s_sc, mx_sc, acc_sc):
    nq = qt_ref.shape[2]
    nk = vt_ref.shape[2]
    tk = vt_ref.shape[4]
    tq = qt_ref.shape[4]
    nslots = s_sc.shape[0]
    ahead = nslots - 1

    n_items = qt_ref.shape[1] * nq

    def put_scores(item, kj):
        hh, qi = item // nq, item % nq
        st = _dot(k_ref[0, hh, kj * tk:(kj + 1) * tk, :], qt_ref[0, hh, qi])
        s_sc[kj % nslots] = st
        mx_sc[kj % nslots] = jnp.max(st, axis=0, keepdims=True)

    for kj in range(ahead):
        put_scores(0, kj)

    def q_tile(item, is_last):
        hh, qi = item // nq, item % nq
        known = min(ahead, nk)
        m = jnp.full((1, tq), NEG, F32)
        for kj in range(known):
            m = jnp.maximum(m, mx_sc[kj % nslots])
        for kj in range(nk):
            cur = kj % nslots
            if kj + ahead < nk:
                put_scores(item, kj + ahead)
            elif not is_last:
                put_scores(item + 1, kj + ahead - nk)
            m_new = m if kj < known else jnp.maximum(m, mx_sc[cur])
            p = jnp.exp2((s_sc[cur] - m_new).astype(BF16))
            pv = _dot(vt_ref[0, hh, kj], p)
            if kj == 0:
                acc_sc[...] = pv
            elif kj < known:
                acc_sc[...] += pv
            else:
                acc_sc[...] = jnp.exp2(m - m_new) * acc_sc[...] + pv
            m = m_new
        acc = acc_sc[...]
        o_ref[0, hh, qi] = (acc[:MLA_V] * (1.0 / acc[MLA_V:MLA_V + 1])).astype(BF16)

    def q_tile_with_lookahead(item, carry):
        q_tile(item, is_last=False)
        return carry

    lax.fori_loop(0, n_items - 1, q_tile_with_lookahead, 0)
    q_tile(n_items - 1, is_last=True)


def _mla_attn(qt, k, vt):
    b, nh, nt, _, ts = qt.shape
    s = k.shape[2]
    nslots = min(MLA_SCORE_SLOTS, nt)
    assert nt % nslots == 0
    hps = MLA_HEADS_PER_STEP
    return pl.pallas_call(
        _mla_attn_kernel,
        grid=(b, nh // hps),
        in_specs=[
            pl.BlockSpec((1, hps, nt, MLA_QK, ts), lambda i, j: (i, j, 0, 0, 0)),
            pl.BlockSpec((1, hps, s, MLA_QK), lambda i, j: (i, j, 0, 0)),
            pl.BlockSpec((1, hps, nt, MLA_V + ONES_ROWS, ts), lambda i, j: (i, j, 0, 0, 0)),
        ],
        out_specs=pl.BlockSpec((1, hps, nt, MLA_V, ts), lambda i, j: (i, j, 0, 0, 0)),
        out_shape=jax.ShapeDtypeStruct((b, nh, nt, MLA_V, ts), BF16),
        scratch_shapes=[
            pltpu.VMEM((nslots, ts, ts), F32),
            pltpu.VMEM((nslots, 1, ts), F32),
            pltpu.VMEM((MLA_V + ONES_ROWS, ts), F32),
        ],
        compiler_params=_params(52 * 2**20, 2),
        name="mla_attn",
    )(qt, k, vt)


def _post_kernel(x_ref, mixt_ref, qmt_ref, sgt_ref, km_ref, vmt_ref, wo_ref, o_ref, s_sc):
    ts = x_ref.shape[1]
    for h in range(MEM_HEADS):
        lo = h * MEM_HEAD_DIM
        s_sc[h] = _dot(km_ref[0, h], qmt_ref[0, lo:lo + MEM_HEAD_DIM, :])
    sg = sgt_ref[0]
    bmix = mixt_ref[...].reshape(MIX_WIDTH, ts) * sg[:MIX_WIDTH]
    out = _dot_tn(bmix, wo_ref[:MIX_WIDTH])
    memo = []
    for h in range(MEM_HEADS):
        st = s_sc[h]
        p = jnp.exp2((st - jnp.max(st, axis=0, keepdims=True)).astype(BF16))
        pv = _dot(vmt_ref[0, h], p)
        memo.append(pv[:MEM_HEAD_DIM] * (1.0 / pv[MEM_HEAD_DIM:MEM_HEAD_DIM + 1]))
    bmem = (jnp.concatenate(memo, axis=0) * sg[MIX_WIDTH:].astype(F32)).astype(BF16)
    o_ref[0] = x_ref[0] + out + _dot_tn(bmem, wo_ref[MIX_WIDTH:])


def _post(x, mixt, mixt_spec, qmt, sgt, km, vmt, wot, ts):
    b, s, _ = x.shape
    return pl.pallas_call(
        _post_kernel,
        grid=(b, s // ts),
        in_specs=[
            pl.BlockSpec((1, ts, D_MODEL), lambda i, j: (i, j, 0)),
            mixt_spec,
            pl.BlockSpec((1, MEM_WIDTH, ts), lambda i, j: (i, 0, j)),
            pl.BlockSpec((1, BRANCH_WIDTH, ts), lambda i, j: (i, 0, j)),
            pl.BlockSpec((1, MEM_HEADS, N_MEM, MEM_HEAD_DIM), lambda i, j: (i, 0, 0, 0)),
            pl.BlockSpec((1, MEM_HEADS, MEM_HEAD_DIM + ONES_ROWS, N_MEM), lambda i, j: (i, 0, 0, 0)),
            _full(wot.shape),
        ],
        out_specs=pl.BlockSpec((1, ts, D_MODEL), lambda i, j: (i, j, 0)),
        out_shape=jax.ShapeDtypeStruct(x.shape, x.dtype),
        scratch_shapes=[pltpu.VMEM((MEM_HEADS, N_MEM, ts), F32)],
        compiler_params=_params(48 * 2**20, 2),
        name="post",
    )(x, mixt, qmt, sgt, km, vmt, wot)


def _proj_b_kernel(x_ref, wbt_ref, gk_ref, qt_ref, k_ref, vt_ref, qmt_ref, sgt_ref):
    for tok in _sub_tiles(x_ref.shape[1], PROJ_B_SUB_TILE):
        h = _normed_input(x_ref, tok)
        _mem_query_and_gate(_dot_nt(wbt_ref[SWA_Q_W + 2 * SWA_KV_W:], h), qmt_ref, sgt_ref, tok)
        qt = _dot_nt(wbt_ref[:SWA_Q_W], h)
        for hh in range(SWA_HEADS):
            lo = hh * SWA_HEAD_DIM
            qh = qt[lo:lo + SWA_HEAD_DIM]
            qn = (qh * (_rms_scale_rows(qh) * SWA_Q_SCALE)).astype(BF16)
            g, hq = divmod(hh, SWA_GROUP)
            for c in range(qn.shape[1] // SWA_K_BLOCK):
                qt_ref[0, g, tok.start // SWA_K_BLOCK + c, :, hq * SWA_K_BLOCK:(hq + 1) * SWA_K_BLOCK] = (
                    qn[:, c * SWA_K_BLOCK:(c + 1) * SWA_K_BLOCK])
        kvt = _dot_nt(wbt_ref[SWA_Q_W:SWA_Q_W + 2 * SWA_KV_W], h)
        kparts = []
        for g in range(SWA_KV_HEADS):
            kh = kvt[g * SWA_HEAD_DIM:(g + 1) * SWA_HEAD_DIM]
            kparts.append(kh * _rms_scale_rows(kh))
        k_ref[0, tok, :] = (jnp.concatenate(kparts, axis=0).T * gk_ref[...]).astype(BF16)
        vt = kvt[SWA_KV_W:].astype(BF16)
        for c in range(vt.shape[1] // SWA_K_BLOCK):
            vt_ref[0, tok.start // SWA_K_BLOCK + c] = vt[:, c * SWA_K_BLOCK:(c + 1) * SWA_K_BLOCK]


def _proj_b(x, wb, ts):
    b, s, _ = x.shape
    consts = [wb["wbt"], wb["gk"]]
    return pl.pallas_call(
        _proj_b_kernel,
        grid=(b, s // ts),
        in_specs=[pl.BlockSpec((1, ts, D_MODEL), lambda i, j: (i, j, 0))] + [_full(c.shape) for c in consts],
        out_specs=[
            pl.BlockSpec((1, SWA_KV_HEADS, ts // SWA_K_BLOCK, SWA_HEAD_DIM, SWA_GROUP * SWA_K_BLOCK),
                         lambda i, j: (i, 0, j, 0, 0)),
            pl.BlockSpec((1, ts, SWA_KV_W), lambda i, j: (i, j, 0)),
            pl.BlockSpec((1, ts // SWA_K_BLOCK, SWA_KV_W, SWA_K_BLOCK), lambda i, j: (i, j, 0, 0)),
            pl.BlockSpec((1, MEM_WIDTH, ts), lambda i, j: (i, 0, j)),
            pl.BlockSpec((1, BRANCH_WIDTH, ts), lambda i, j: (i, 0, j)),
        ],
        out_shape=[
            jax.ShapeDtypeStruct((b, SWA_KV_HEADS, s // SWA_K_BLOCK, SWA_HEAD_DIM, SWA_GROUP * SWA_K_BLOCK), BF16),
            jax.ShapeDtypeStruct((b, s, SWA_KV_W), BF16),
            jax.ShapeDtypeStruct((b, s // SWA_K_BLOCK, SWA_KV_W, SWA_K_BLOCK), BF16),
            jax.ShapeDtypeStruct((b, MEM_WIDTH, s), BF16),
            jax.ShapeDtypeStruct((b, BRANCH_WIDTH, s), BF16),
        ],
        compiler_params=_params(56 * 2**20, 2),
        name="proj_b",
    )(x, *consts)


def _alibi_slope(h):
    return 2.0 ** (-8.0 * (h + 1) / SWA_HEADS)


def _swa_kernel(qt_ref, k_ref, vt_ref, bias_lo_ref, bias_mid_ref, bias_hi_ref, sink_ref, o_ref, s_sc, mx_sc):
    per_tile = qt_ref.shape[2]
    n_blocks = vt_ref.shape[1]
    first = pl.program_id(1) * per_tile - SWA_WINDOW_BLOCKS // 2
    blocks = [jnp.clip(first + o, 0, n_blocks - 1) for o in range(per_tile + SWA_WINDOW_BLOCKS)]
    k_all = jnp.concatenate(
        [k_ref[0, pl.ds(pl.multiple_of(blk * SWA_K_BLOCK, SWA_K_BLOCK), SWA_K_BLOCK), :] for blk in blocks],
        axis=0)
    vt_all = jnp.concatenate([vt_ref[0, blk] for blk in blocks], axis=1)
    bias_refs = (bias_lo_ref,) + (bias_mid_ref,) * (per_tile - 2) + (bias_hi_ref,)
    chains = [(sb, g, c) for sb in range(per_tile) for g in range(SWA_KV_HEADS)
              for c in range(SWA_GROUP // SWA_HEADS_PER_CHAIN)]
    k_g, vt_g = {}, {}
    for sb in range(len(bias_refs)):
        k0 = sb * SWA_K_BLOCK
        for g in range(SWA_KV_HEADS):
            k_g[sb, g] = k_all[k0:k0 + SWA_BAND_KEYS, g * SWA_HEAD_DIM:(g + 1) * SWA_HEAD_DIM]
            vt_g[sb, g] = jnp.concatenate(
                [vt_all[g * SWA_HEAD_DIM:(g + 1) * SWA_HEAD_DIM, k0:k0 + SWA_BAND_KEYS],
                 jnp.ones((ONES_ROWS, SWA_BAND_KEYS), BF16)], axis=0)

    def put_scores(slot, chain):
        sb, g, c = chain
        lanes = slice(c * SWA_CHAIN_LANES, (c + 1) * SWA_CHAIN_LANES)
        st = _dot(k_g[sb, g], qt_ref[0, g, sb, :, lanes]) + bias_refs[sb][0, g, :, lanes]
        s_sc[slot] = st
        mx_sc[slot] = jnp.max(st, axis=0, keepdims=True)

    nslots = s_sc.shape[0]
    ahead = nslots - 1
    for t in range(min(ahead, len(chains))):
        put_scores(t, chains[t])
    for t, (sb, g, c) in enumerate(chains):
        if t + ahead < len(chains):
            put_scores((t + ahead) % nslots, chains[t + ahead])
        sink = sink_ref[g, :, c * SWA_CHAIN_LANES:(c + 1) * SWA_CHAIN_LANES]
        m = jnp.maximum(mx_sc[t % nslots], sink)
        p = jnp.exp2((s_sc[t % nslots] - m).astype(BF16))
        pv = _dot(vt_g[sb, g], p)
        l = pv[SWA_HEAD_DIM:SWA_HEAD_DIM + 1] + jnp.exp2(sink - m)
        ot = pv[:SWA_HEAD_DIM] * (1.0 / l)
        for hh in range(SWA_HEADS_PER_CHAIN):
            lo = ((g * SWA_GROUP) + c * SWA_HEADS_PER_CHAIN + hh) * SWA_HEAD_DIM
            o_ref[0, lo:lo + SWA_HEAD_DIM, sb * SWA_K_BLOCK:(sb + 1) * SWA_K_BLOCK] = (
                ot[:, hh * SWA_K_BLOCK:(hh + 1) * SWA_K_BLOCK].astype(BF16))


def _swa_attn(qt, k, vt, bias, sink):
    b, s = k.shape[0], k.shape[1]
    tq = min(SWA_Q_TILE, s)
    per_tile = tq // SWA_K_BLOCK
    assert per_tile >= 2
    last_tile = s // tq - 1
    bias_block = (1,) + bias.shape[1:]
    return pl.pallas_call(
        _swa_kernel,
        grid=(b, s // tq),
        in_specs=(
            [pl.BlockSpec((1, SWA_KV_HEADS, per_tile, SWA_HEAD_DIM, SWA_GROUP * SWA_K_BLOCK),
                          lambda i, j: (i, 0, j, 0, 0))]
            + [pl.BlockSpec((1,) + k.shape[1:], lambda i, j: (i, 0, 0)),
               pl.BlockSpec((1,) + vt.shape[1:], lambda i, j: (i, 0, 0, 0))]
            + [pl.BlockSpec(bias_block, lambda i, j: (jnp.where(j == 0, 0, 1), 0, 0, 0)),
               pl.BlockSpec(bias_block, lambda i, j: (1, 0, 0, 0)),
               pl.BlockSpec(bias_block, lambda i, j: (jnp.where(j == last_tile, 2, 1), 0, 0, 0)),
               _full(sink.shape)]
        ),
        out_specs=pl.BlockSpec((1, SWA_Q_W, tq), lambda i, j: (i, 0, j)),
        out_shape=jax.ShapeDtypeStruct((b, SWA_Q_W, s), BF16),
        scratch_shapes=[pltpu.VMEM((SWA_SCORE_SLOTS, SWA_BAND_KEYS, SWA_CHAIN_LANES), F32),
                        pltpu.VMEM((SWA_SCORE_SLOTS, 1, SWA_CHAIN_LANES), F32)],
        compiler_params=_params(48 * 2**20, 2),
        name="swa_attn",
    )(qt, k, vt, bias, bias, bias, sink)


def _col(v):
    return v.astype(F32)[:, None]


def _row(v):
    return v.astype(F32)[None, :]


def _prep_layer_a(seq, norm_in, a_w_in, a_q_a_norm, a_w_q_b, a_kv_a_norm, a_w_kv_b, a_q_norm, a_k_norm):
    kr_lo = MLA_Q_RANK + MLA_KV_RANK
    qm_lo = kr_lo + MLA_ROPE
    w_in = _col(norm_in) * a_w_in
    kr_cols = w_in[:, kr_lo:qm_lo]
    wkv = (_col(a_kv_a_norm) * a_w_kv_b).reshape(MLA_KV_RANK, MLA_HEADS, MLA_NOPE + MLA_V)
    inv = 1.0 / (ROPE_THETA ** (np.arange(0, MLA_ROPE, 2, dtype=np.float64) / MLA_ROPE))
    ang = np.arange(seq, dtype=np.float64)[:, None] * inv[None, :]
    cos, sin = np.cos(ang).astype(np.float32), np.sin(ang).astype(np.float32)
    g1 = _col(a_q_norm[MLA_NOPE:MLA_NOPE + MLA_HALF]) * MLA_Q_SCALE
    g2 = _col(a_q_norm[MLA_NOPE + MLA_HALF:]) * MLA_Q_SCALE
    return {
        "w1": jnp.concatenate([w_in[:, :kr_lo], kr_cols, kr_cols], axis=1).astype(BF16),
        "wgt": w_in[:, qm_lo:].T.astype(BF16),
        "wqbt": (_col(a_q_a_norm) * a_w_q_b).T.astype(BF16),
        "wkn": wkv[:, :, :MLA_NOPE].reshape(MLA_KV_RANK, MLA_HEADS * MLA_NOPE).astype(BF16),
        "wvt": wkv[:, :, MLA_NOPE:].reshape(MLA_KV_RANK, MLA_HEADS * MLA_V).T.astype(BF16),
        "qtab": jnp.concatenate([g1 * cos.T, g2 * sin.T, g2 * cos.T, g1 * sin.T], axis=0),
        "gkn": _row(a_k_norm[:MLA_NOPE] * a_q_norm[:MLA_NOPE]),
        "gkr": _row(jnp.tile(a_k_norm[MLA_NOPE:], 2)),
        "ctab": jnp.asarray(np.tile(cos, (1, 4))),
        "stab": jnp.asarray(np.concatenate([-sin, sin, -sin, sin], axis=1)),
    }


def _prep_layer_b(norm_in, b_w_in, b_q_norm, b_k_norm):
    return {
        "wbt": (_col(norm_in) * b_w_in).T.astype(BF16),
        "gk": _row(jnp.tile(b_k_norm * b_q_norm, SWA_KV_HEADS)),
    }


def _prep_mem(mem_norm, w_mem_kv, mem_k_norm, mem_q_norm):
    w = _col(mem_norm) * w_mem_kv
    return {
        "wk": w[:, :MEM_WIDTH].astype(BF16),
        "wvt": w[:, MEM_WIDTH:].T.astype(BF16),
        "gk": _row(mem_k_norm * mem_q_norm),
    }


def _swa_bias():
    r = np.arange(SWA_BAND_KEYS)[:, None]
    c = np.arange(SWA_K_BLOCK)[None, :]
    rel = r - WINDOW - c
    dist = np.abs(rel).astype(np.float64)
    slopes = np.array([_alibi_slope(h) for h in range(SWA_HEADS)]).reshape(SWA_KV_HEADS, SWA_GROUP)
    alibi = -(LOG2E * slopes)[:, None, :, None] * dist[None, :, None, :]
    in_window = (np.abs(rel) <= WINDOW)[None, None, :, None, :]
    key_block = (r // SWA_K_BLOCK)[None, None, :, None, :]
    variant = np.arange(3)[:, None, None, None, None]
    in_seq = ~(((variant == 0) & (key_block == 0)) | ((variant == 2) & (key_block == 2)))
    bias = np.where(in_window & in_seq, alibi[None], NEG).astype(np.float32)
    return jnp.asarray(bias.reshape(3, SWA_KV_HEADS, SWA_BAND_KEYS, SWA_GROUP * SWA_K_BLOCK))


def _swa_sink_lanes(b_sink):
    s = (b_sink.astype(F32) * LOG2E).reshape(SWA_KV_HEADS, 1, SWA_GROUP, 1)
    return jnp.broadcast_to(s, (SWA_KV_HEADS, 1, SWA_GROUP, SWA_K_BLOCK)).reshape(SWA_KV_HEADS, 1, -1)


def _trunk(x, mem, wa, wb, wmem, wots, sink, bias):
    ts = min(SEQ_TILE, x.shape[1])
    b = x.shape[0]
    nt = x.shape[1] // ts

    km, vmt = _memkv(mem, **wmem[0])
    qt, k, vt, qmt, sgt = _proj_a(x, wa, ts)
    mixt = _mla_attn(qt, k, vt)
    mix_spec = pl.BlockSpec((1, MLA_HEADS, 1, MLA_V, ts), lambda i, j: (i, 0, j, 0, 0))
    x = _post(x, mixt, mix_spec, qmt, sgt, km, vmt, wots[0], ts)

    km, vmt = _memkv(mem, **wmem[1])
    qt, k, vt, qmt, sgt = _proj_b(x, wb, min(PROJ_B_TILE, x.shape[1]))
    mixt = _swa_attn(qt, k, vt, bias, sink)
    mix_spec = pl.BlockSpec((1, MIX_WIDTH, ts), lambda i, j: (i, 0, j))
    return _post(x, mixt, mix_spec, qmt, sgt, km, vmt, wots[1], ts)


def kernel(x_prompt, x_sample, mem_prompt, mem_sample, norm_in, w_out, mem_norm, w_mem_kv, mem_q_norm, mem_k_norm, a_w_in, a_q_a_norm, a_w_q_b, a_kv_a_norm, a_w_kv_b, a_q_norm, a_k_norm, b_w_in, b_q_norm, b_k_norm, b_sink):
    assert norm_in.shape[0] == 2 and a_w_in.shape[0] == 1 and b_w_in.shape[0] == 1
    assert x_prompt.shape[1] == x_sample.shape[1]
    seq = x_prompt.shape[1]
    assert seq % min(SWA_Q_TILE, seq) == 0 and seq % min(SEQ_TILE, seq) == 0
    wa = _prep_layer_a(seq, norm_in[0], a_w_in[0], a_q_a_norm[0], a_w_q_b[0], a_kv_a_norm[0], a_w_kv_b[0],
                       a_q_norm[0], a_k_norm[0])
    wb = _prep_layer_b(norm_in[1], b_w_in[0], b_q_norm[0], b_k_norm[0])
    wmem = [_prep_mem(mem_norm[i], w_mem_kv[i], mem_k_norm[i], mem_q_norm[i]) for i in range(2)]
    wots = [w_out[i].astype(BF16) for i in range(2)]
    sink = _swa_sink_lanes(b_sink[0])
    bias = _swa_bias()
    y_prompt = _trunk(x_prompt, mem_prompt, wa, wb, wmem, wots, sink, bias)
    y_sample = _trunk(x_sample, mem_sample, wa, wb, wmem, wots, sink, bias)
    return (y_prompt, y_sample)
```

```python
import math

import jax
import jax.numpy as jnp
import numpy as np
from jax import lax
from jax.experimental import pallas as pl
from jax.experimental.pallas import tpu as pltpu

D_MODEL = 1024
N_MEM = 256
MEM_HEADS = 4
MEM_HEAD_DIM = 128
MEM_WIDTH = MEM_HEADS * MEM_HEAD_DIM

MLA_HEADS = 8
MLA_Q_RANK = 384
MLA_KV_RANK = 256
MLA_NOPE = 128
MLA_ROPE = 64
MLA_HALF = MLA_ROPE // 2
MLA_V = 128
MLA_QK = MLA_NOPE + MLA_ROPE
ROPE_THETA = 10000.0

SWA_HEADS = 16
SWA_KV_HEADS = 2
SWA_HEAD_DIM = 64
SWA_GROUP = SWA_HEADS // SWA_KV_HEADS
SWA_Q_W = SWA_HEADS * SWA_HEAD_DIM
SWA_KV_W = SWA_KV_HEADS * SWA_HEAD_DIM
WINDOW = 128

MIX_WIDTH = 1024
BRANCH_WIDTH = MIX_WIDTH + MEM_WIDTH
EPS = 1e-6
NEG = -1e30

V7X_LANES = 128
V7X_VMEM_BYTES = 64 * 1024 * 1024

SEQ_TILE = 512
PROJ_A_SUB_TILE = 512
PROJ_B_TILE = 1024
PROJ_B_SUB_TILE = 256
MLA_SCORE_SLOTS = 8
MLA_HEADS_PER_STEP = 4
SWA_Q_TILE = 2048
SWA_K_BLOCK = 128
SWA_WINDOW_BLOCKS = 2 * WINDOW // SWA_K_BLOCK
SWA_BAND_KEYS = SWA_K_BLOCK + 2 * WINDOW
SWA_HEADS_PER_CHAIN = 2
SWA_CHAIN_LANES = SWA_HEADS_PER_CHAIN * SWA_K_BLOCK
SWA_SCORE_SLOTS = 5
LOG2E = math.log2(math.e)
MLA_Q_SCALE = MLA_QK ** -0.5 * LOG2E
SWA_Q_SCALE = SWA_HEAD_DIM ** -0.5 * LOG2E
MEM_Q_SCALE = MEM_HEAD_DIM ** -0.5 * LOG2E
ONES_ROWS = 16

F32 = jnp.float32
BF16 = jnp.bfloat16

_NT = (((1,), (1,)), ((), ()))
_TN = (((0,), (0,)), ((), ()))


def _vmem_limit(nbytes):
    return int(min(nbytes, V7X_VMEM_BYTES - 8 * 1024 * 1024))


def _params(nbytes, ndims):
    return pltpu.CompilerParams(
        dimension_semantics=("arbitrary",) * ndims,
        vmem_limit_bytes=_vmem_limit(nbytes))


def _dot(a, b):
    return jnp.dot(a, b, preferred_element_type=F32)


def _dot_nt(a, b):
    return lax.dot_general(a, b, _NT, preferred_element_type=F32)


def _dot_tn(a, b):
    return lax.dot_general(a, b, _TN, preferred_element_type=F32)


def _rms_lanes(x):
    return x * lax.rsqrt(jnp.mean(x * x, axis=-1, keepdims=True) + EPS)


def _rms_scale_rows(x):
    return lax.rsqrt(jnp.mean(x * x, axis=0, keepdims=True) + EPS)


def _full(shape):
    return pl.BlockSpec(shape, lambda *_: (0,) * len(shape))


def _memkv_kernel(mem_ref, wk_ref, wvt_ref, gk_ref, km_ref, vmt_ref):
    mn = _rms_lanes(mem_ref[0]).astype(BF16)
    k = _dot(mn, wk_ref[...])
    vt = _dot_nt(wvt_ref[...], mn)
    for h in range(MEM_HEADS):
        lo = h * MEM_HEAD_DIM
        kh = k[:, lo:lo + MEM_HEAD_DIM]
        km_ref[0, h] = (_rms_lanes(kh) * gk_ref[...]).astype(BF16)
        vmt_ref[0, h, :MEM_HEAD_DIM, :] = vt[lo:lo + MEM_HEAD_DIM].astype(BF16)
        vmt_ref[0, h, MEM_HEAD_DIM:, :] = jnp.ones((ONES_ROWS, N_MEM), BF16)


def _memkv(mem, wk, wvt, gk):
    b = mem.shape[0]
    return pl.pallas_call(
        _memkv_kernel,
        grid=(b,),
        in_specs=[
            pl.BlockSpec((1, N_MEM, D_MODEL), lambda i: (i, 0, 0)),
            _full((D_MODEL, MEM_WIDTH)),
            _full((MEM_WIDTH, D_MODEL)),
            _full((1, MEM_HEAD_DIM)),
        ],
        out_specs=[
            pl.BlockSpec((1, MEM_HEADS, N_MEM, MEM_HEAD_DIM), lambda i: (i, 0, 0, 0)),
            pl.BlockSpec((1, MEM_HEADS, MEM_HEAD_DIM + ONES_ROWS, N_MEM), lambda i: (i, 0, 0, 0)),
        ],
        out_shape=[
            jax.ShapeDtypeStruct((b, MEM_HEADS, N_MEM, MEM_HEAD_DIM), BF16),
            jax.ShapeDtypeStruct((b, MEM_HEADS, MEM_HEAD_DIM + ONES_ROWS, N_MEM), BF16),
        ],
        compiler_params=_params(24 * 2**20, 1),
        name="memkv",
    )(mem, wk, wvt, gk)


def _normed_input(x_ref, tok):
    return _rms_lanes(x_ref[0, tok, :]).astype(BF16)


def _mem_query_and_gate(zt, qmt_ref, sgt_ref, tok):
    for h in range(MEM_HEADS):
        lo = h * MEM_HEAD_DIM
        qm = zt[lo:lo + MEM_HEAD_DIM]
        qmt_ref[0, lo:lo + MEM_HEAD_DIM, tok] = (qm * (_rms_scale_rows(qm) * MEM_Q_SCALE)).astype(BF16)
    hg = 0.5 * zt[MEM_WIDTH:]
    sgt_ref[0, :, tok] = (hg + hg * jnp.tanh(hg)).astype(BF16)


def _sub_tiles(ts, sub):
    sub = min(sub, ts)
    return [slice(c * sub, (c + 1) * sub) for c in range(ts // sub)]


def _proj_a_kernel(x_ref, w1_ref, wgt_ref, wqbt_ref, wkn_ref, wvt_ref,
                   qtab_ref, gkn_ref, gkr_ref, ctab_ref, stab_ref,
                   qt_ref, k_ref, vt_ref, qmt_ref, sgt_ref):
    kv_lo = MLA_Q_RANK
    kr_lo = MLA_Q_RANK + MLA_KV_RANK
    for tok in _sub_tiles(x_ref.shape[1], PROJ_A_SUB_TILE):
        h = _normed_input(x_ref, tok)
        z1 = _dot(h, w1_ref[...])
        _mem_query_and_gate(_dot_nt(wgt_ref[...], h), qmt_ref, sgt_ref, tok)
        cqn = _rms_lanes(z1[:, :kv_lo]).astype(BF16)
        ckvn = _rms_lanes(z1[:, kv_lo:kr_lo]).astype(BF16)
        kr = z1[:, kr_lo:]

        qt = _dot_nt(wqbt_ref[...], cqn)
        a1 = qtab_ref[0 * MLA_HALF:1 * MLA_HALF, tok]
        b1 = qtab_ref[1 * MLA_HALF:2 * MLA_HALF, tok]
        a2 = qtab_ref[2 * MLA_HALF:3 * MLA_HALF, tok]
        b2 = qtab_ref[3 * MLA_HALF:4 * MLA_HALF, tok]
        for hh in range(MLA_HEADS):
            qh = qt[hh * MLA_QK:(hh + 1) * MLA_QK]
            r = _rms_scale_rows(qh)
            x1 = qh[MLA_NOPE:MLA_NOPE + MLA_HALF]
            x2 = qh[MLA_NOPE + MLA_HALF:]
            qt_ref[0, hh, 0, :MLA_NOPE, tok] = (qh[:MLA_NOPE] * (r * MLA_Q_SCALE)).astype(BF16)
            qt_ref[0, hh, 0, MLA_NOPE:MLA_NOPE + MLA_HALF, tok] = ((x1 * a1 - x2 * b1) * r).astype(BF16)
            qt_ref[0, hh, 0, MLA_NOPE + MLA_HALF:, tok] = ((x2 * a2 + x1 * b2) * r).astype(BF16)

        kn = _dot(ckvn, wkn_ref[...])
        vt = _dot_nt(wvt_ref[...], ckvn)
        krg = kr * gkr_ref[...]
        kro = krg * ctab_ref[tok, :] + pltpu.roll(krg, MLA_HALF, 1) * stab_ref[tok, :]
        ss_r = 0.5 * jnp.sum(kr * kr, axis=1, keepdims=True)
        for hh in range(MLA_HEADS):
            knh = kn[:, hh * MLA_NOPE:(hh + 1) * MLA_NOPE]
            ss = jnp.sum(knh * knh, axis=1, keepdims=True) + ss_r
            r = lax.rsqrt(ss * (1.0 / MLA_QK) + EPS)
            k_ref[0, hh, tok, :MLA_NOPE] = (knh * gkn_ref[...] * r).astype(BF16)
            k_ref[0, hh, tok, MLA_NOPE:] = (kro[:, :MLA_ROPE] * r).astype(BF16)
            vt_ref[0, hh, 0, :MLA_V, tok] = vt[hh * MLA_V:(hh + 1) * MLA_V].astype(BF16)
            vt_ref[0, hh, 0, MLA_V:, tok] = jnp.ones((ONES_ROWS, vt.shape[1]), BF16)


def _proj_a(x, wa, ts):
    b, s, _ = x.shape
    nt = s // ts
    consts = [wa["w1"], wa["wgt"], wa["wqbt"], wa["wkn"], wa["wvt"]]
    tail = [wa["gkn"], wa["gkr"]]
    in_specs = (
        [pl.BlockSpec((1, ts, D_MODEL), lambda i, j: (i, j, 0))]
        + [_full(c.shape) for c in consts]
        + [pl.BlockSpec((4 * MLA_HALF, ts), lambda i, j: (0, j))]
        + [_full(c.shape) for c in tail]
        + [pl.BlockSpec((ts, V7X_LANES), lambda i, j: (j, 0))] * 2
    )
    out_shape = [
        jax.ShapeDtypeStruct((b, MLA_HEADS, nt, MLA_QK, ts), BF16),
        jax.ShapeDtypeStruct((b, MLA_HEADS, s, MLA_QK), BF16),
        jax.ShapeDtypeStruct((b, MLA_HEADS, nt, MLA_V + ONES_ROWS, ts), BF16),
        jax.ShapeDtypeStruct((b, MEM_WIDTH, s), BF16),
        jax.ShapeDtypeStruct((b, BRANCH_WIDTH, s), BF16),
    ]
    out_specs = [
        pl.BlockSpec((1, MLA_HEADS, 1, MLA_QK, ts), lambda i, j: (i, 0, j, 0, 0)),
        pl.BlockSpec((1, MLA_HEADS, ts, MLA_QK), lambda i, j: (i, 0, j, 0)),
        pl.BlockSpec((1, MLA_HEADS, 1, MLA_V + ONES_ROWS, ts), lambda i, j: (i, 0, j, 0, 0)),
        pl.BlockSpec((1, MEM_WIDTH, ts), lambda i, j: (i, 0, j)),
        pl.BlockSpec((1, BRANCH_WIDTH, ts), lambda i, j: (i, 0, j)),
    ]
    return pl.pallas_call(
        _proj_a_kernel,
        grid=(b, nt),
        in_specs=in_specs,
        out_specs=out_specs,
        out_shape=out_shape,
        compiler_params=_params(56 * 2**20, 2),
        name="proj_a",
    )(x, *consts, wa["qtab"], *tail, wa["ctab"], wa["stab"])


def _mla_attn_kernel(qt_ref, k_ref, vt_ref, o_ref, s_sc, mx_sc, acc_sc):
    nq = qt_ref.shape[2]
    nk = vt_ref.shape[2]
    tk = vt_ref.shape[4]
    tq = qt_ref.shape[4]
    nslots = s_sc.shape[0]
    ahead = nslots - 1

    n_items = qt_ref.shape[1] * nq

    def put_scores(item, kj):
        hh, qi = item // nq, item % nq
        st = _dot(k_ref[0, hh, kj * tk:(kj + 1) * tk, :], qt_ref[0, hh, qi])
        s_sc[kj % nslots] = st
        mx_sc[kj % nslots] = jnp.max(st, axis=0, keepdims=True)

    for kj in range(ahead):
        put_scores(0, kj)

    def q_tile(item, is_last):
        hh, qi = item // nq, item % nq
        known = min(ahead, nk)
        m = jnp.full((1, tq), NEG, F32)
        for kj in range(known):
            m = jnp.maximum(m, mx_sc[kj % nslots])
        for kj in range(nk):
            cur = kj % nslots
            if kj + ahead < nk:
                put_scores(item, kj + ahead)
            elif not is_last:
                put_scores(item + 1, kj + ahead - nk)
            m_new = m if kj < known else jnp.maximum(m, mx_sc[cur])
            p = jnp.exp2((s_sc[cur] - m_new).astype(BF16))
            pv = _dot(vt_ref[0, hh, kj], p)
            if kj == 0:
                acc_sc[...] = pv
            elif kj < known:
                acc_sc[...] += pv
            else:
                acc_sc[...] = jnp.exp2(m - m_new) * acc_sc[...] + pv
            m = m_new
        acc = acc_sc[...]
        o_ref[0, hh, qi] = (acc[:MLA_V] * (1.0 / acc[MLA_V:MLA_V + 1])).astype(BF16)

    def q_tile_with_lookahead(item, carry):
        q_tile(item, is_last=False)
        return carry

    lax.fori_loop(0, n_items - 1, q_tile_with_lookahead, 0)
    q_tile(n_items - 1, is_last=True)


def _mla_attn(qt, k, vt):
    b, nh, nt, _, ts = qt.shape
    s = k.shape[2]
    nslots = min(MLA_SCORE_SLOTS, nt)
    assert nt % nslots == 0
    hps = MLA_HEADS_PER_STEP
    return pl.pallas_call(
        _mla_attn_kernel,
        grid=(b, nh // hps),
        in_specs=[
            pl.BlockSpec((1, hps, nt, MLA_QK, ts), lambda i, j: (i, j, 0, 0, 0)),
            pl.BlockSpec((1, hps, s, MLA_QK), lambda i, j: (i, j, 0, 0)),
            pl.BlockSpec((1, hps, nt, MLA_V + ONES_ROWS, ts), lambda i, j: (i, j, 0, 0, 0)),
        ],
        out_specs=pl.BlockSpec((1, hps, nt, MLA_V, ts), lambda i, j: (i, j, 0, 0, 0)),
        out_shape=jax.ShapeDtypeStruct((b, nh, nt, MLA_V, ts), BF16),
        scratch_shapes=[
            pltpu.VMEM((nslots, ts, ts), F32),
            pltpu.VMEM((nslots, 1, ts), F32),
            pltpu.VMEM((MLA_V + ONES_ROWS, ts), F32),
        ],
        compiler_params=_params(56 * 2**20, 2),
        name="mla_attn",
    )(qt, k, vt)


def _post_kernel(x_ref, mixt_ref, qmt_ref, sgt_ref, km_ref, vmt_ref, wo_ref, o_ref, s_sc):
    ts = x_ref.shape[1]
    for h in range(MEM_HEADS):
        lo = h * MEM_HEAD_DIM
        s_sc[h] = _dot(km_ref[0, h], qmt_ref[0, lo:lo + MEM_HEAD_DIM, :])
    sg = sgt_ref[0]
    bmix = mixt_ref[...].reshape(MIX_WIDTH, ts) * sg[:MIX_WIDTH]
    out = _dot_tn(bmix, wo_ref[:MIX_WIDTH])
    memo = []
    for h in range(MEM_HEADS):
        st = s_sc[h]
        p = jnp.exp2((st - jnp.max(st, axis=0, keepdims=True)).astype(BF16))
        pv = _dot(vmt_ref[0, h], p)
        memo.append(pv[:MEM_HEAD_DIM] * (1.0 / pv[MEM_HEAD_DIM:MEM_HEAD_DIM + 1]))
    bmem = (jnp.concatenate(memo, axis=0) * sg[MIX_WIDTH:].astype(F32)).astype(BF16)
    o_ref[0] = x_ref[0] + out + _dot_tn(bmem, wo_ref[MIX_WIDTH:])


def _post(x, mixt, mixt_spec, qmt, sgt, km, vmt, wot, ts):
    b, s, _ = x.shape
    return pl.pallas_call(
        _post_kernel,
        grid=(b, s // ts),
        in_specs=[
            pl.BlockSpec((1, ts, D_MODEL), lambda i, j: (i, j, 0)),
            mixt_spec,
            pl.BlockSpec((1, MEM_WIDTH, ts), lambda i, j: (i, 0, j)),
            pl.BlockSpec((1, BRANCH_WIDTH, ts), lambda i, j: (i, 0, j)),
            pl.BlockSpec((1, MEM_HEADS, N_MEM, MEM_HEAD_DIM), lambda i, j: (i, 0, 0, 0)),
            pl.BlockSpec((1, MEM_HEADS, MEM_HEAD_DIM + ONES_ROWS, N_MEM), lambda i, j: (i, 0, 0, 0)),
            _full(wot.shape),
        ],
        out_specs=pl.BlockSpec((1, ts, D_MODEL), lambda i, j: (i, j, 0)),
        out_shape=jax.ShapeDtypeStruct(x.shape, x.dtype),
        scratch_shapes=[pltpu.VMEM((MEM_HEADS, N_MEM, ts), F32)],
        compiler_params=_params(48 * 2**20, 2),
        name="post",
    )(x, mixt, qmt, sgt, km, vmt, wot)


def _proj_b_kernel(x_ref, wbt_ref, gk_ref, qt_ref, k_ref, vt_ref, qmt_ref, sgt_ref):
    for tok in _sub_tiles(x_ref.shape[1], PROJ_B_SUB_TILE):
        h = _normed_input(x_ref, tok)
        _mem_query_and_gate(_dot_nt(wbt_ref[SWA_Q_W + 2 * SWA_KV_W:], h), qmt_ref, sgt_ref, tok)
        qt = _dot_nt(wbt_ref[:SWA_Q_W], h)
        for hh in range(SWA_HEADS):
            lo = hh * SWA_HEAD_DIM
            qh = qt[lo:lo + SWA_HEAD_DIM]
            qn = (qh * (_rms_scale_rows(qh) * SWA_Q_SCALE)).astype(BF16)
            g, hq = divmod(hh, SWA_GROUP)
            for c in range(qn.shape[1] // SWA_K_BLOCK):
                qt_ref[0, g, tok.start // SWA_K_BLOCK + c, :, hq * SWA_K_BLOCK:(hq + 1) * SWA_K_BLOCK] = (
                    qn[:, c * SWA_K_BLOCK:(c + 1) * SWA_K_BLOCK])
        kvt = _dot_nt(wbt_ref[SWA_Q_W:SWA_Q_W + 2 * SWA_KV_W], h)
        kparts = []
        for g in range(SWA_KV_HEADS):
            kh = kvt[g * SWA_HEAD_DIM:(g + 1) * SWA_HEAD_DIM]
            kparts.append(kh * _rms_scale_rows(kh))
        k_ref[0, tok, :] = (jnp.concatenate(kparts, axis=0).T * gk_ref[...]).astype(BF16)
        vt = kvt[SWA_KV_W:].astype(BF16)
        for c in range(vt.shape[1] // SWA_K_BLOCK):
            vt_ref[0, tok.start // SWA_K_BLOCK + c] = vt[:, c * SWA_K_BLOCK:(c + 1) * SWA_K_BLOCK]


def _proj_b(x, wb, ts):
    b, s, _ = x.shape
    consts = [wb["wbt"], wb["gk"]]
    return pl.pallas_call(
        _proj_b_kernel,
        grid=(b, s // ts),
        in_specs=[pl.BlockSpec((1, ts, D_MODEL), lambda i, j: (i, j, 0))] + [_full(c.shape) for c in consts],
        out_specs=[
            pl.BlockSpec((1, SWA_KV_HEADS, ts // SWA_K_BLOCK, SWA_HEAD_DIM, SWA_GROUP * SWA_K_BLOCK),
                         lambda i, j: (i, 0, j, 0, 0)),
            pl.BlockSpec((1, ts, SWA_KV_W), lambda i, j: (i, j, 0)),
            pl.BlockSpec((1, ts // SWA_K_BLOCK, SWA_KV_W, SWA_K_BLOCK), lambda i, j: (i, j, 0, 0)),
            pl.BlockSpec((1, MEM_WIDTH, ts), lambda i, j: (i, 0, j)),
            pl.BlockSpec((1, BRANCH_WIDTH, ts), lambda i, j: (i, 0, j)),
        ],
        out_shape=[
            jax.ShapeDtypeStruct((b, SWA_KV_HEADS, s // SWA_K_BLOCK, SWA_HEAD_DIM, SWA_GROUP * SWA_K_BLOCK), BF16),
            jax.ShapeDtypeStruct((b, s, SWA_KV_W), BF16),
            jax.ShapeDtypeStruct((b, s // SWA_K_BLOCK, SWA_KV_W, SWA_K_BLOCK), BF16),
            jax.ShapeDtypeStruct((b, MEM_WIDTH, s), BF16),
            jax.ShapeDtypeStruct((b, BRANCH_WIDTH, s), BF16),
        ],
        compiler_params=_params(56 * 2**20, 2),
        name="proj_b",
    )(x, *consts)


def _alibi_slope(h):
    return 2.0 ** (-8.0 * (h + 1) / SWA_HEADS)


def _swa_kernel(qt_ref, k_ref, vt_ref, bias_lo_ref, bias_mid_ref, bias_hi_ref, sink_ref, o_ref, s_sc, mx_sc):
    per_tile = qt_ref.shape[2]
    n_blocks = vt_ref.shape[1]
    first = pl.program_id(1) * per_tile - SWA_WINDOW_BLOCKS // 2
    blocks = [jnp.clip(first + o, 0, n_blocks - 1) for o in range(per_tile + SWA_WINDOW_BLOCKS)]
    k_all = jnp.concatenate(
        [k_ref[0, pl.ds(pl.multiple_of(blk * SWA_K_BLOCK, SWA_K_BLOCK), SWA_K_BLOCK), :] for blk in blocks],
        axis=0)
    vt_all = jnp.concatenate([vt_ref[0, blk] for blk in blocks], axis=1)
    bias_refs = (bias_lo_ref,) + (bias_mid_ref,) * (per_tile - 2) + (bias_hi_ref,)
    chains = [(sb, g, c) for sb in range(per_tile) for g in range(SWA_KV_HEADS)
              for c in range(SWA_GROUP // SWA_HEADS_PER_CHAIN)]
    k_g, vt_g = {}, {}
    for sb in range(len(bias_refs)):
        k0 = sb * SWA_K_BLOCK
        for g in range(SWA_KV_HEADS):
            k_g[sb, g] = k_all[k0:k0 + SWA_BAND_KEYS, g * SWA_HEAD_DIM:(g + 1) * SWA_HEAD_DIM]
            vt_g[sb, g] = jnp.concatenate(
                [vt_all[g * SWA_HEAD_DIM:(g + 1) * SWA_HEAD_DIM, k0:k0 + SWA_BAND_KEYS],
                 jnp.ones((ONES_ROWS, SWA_BAND_KEYS), BF16)], axis=0)

    def put_scores(slot, chain):
        sb, g, c = chain
        lanes = slice(c * SWA_CHAIN_LANES, (c + 1) * SWA_CHAIN_LANES)
        st = _dot(k_g[sb, g], qt_ref[0, g, sb, :, lanes]) + bias_refs[sb][0, g, :, lanes]
        s_sc[slot] = st
        mx_sc[slot] = jnp.max(st, axis=0, keepdims=True)

    nslots = s_sc.shape[0]
    ahead = nslots - 1
    for t in range(min(ahead, len(chains))):
        put_scores(t, chains[t])
    for t, (sb, g, c) in enumerate(chains):
        if t + ahead < len(chains):
            put_scores((t + ahead) % nslots, chains[t + ahead])
        sink = sink_ref[g, :, c * SWA_CHAIN_LANES:(c + 1) * SWA_CHAIN_LANES]
        m = jnp.maximum(mx_sc[t % nslots], sink)
        p = jnp.exp2((s_sc[t % nslots] - m).astype(BF16))
        pv = _dot(vt_g[sb, g], p)
        l = pv[SWA_HEAD_DIM:SWA_HEAD_DIM + 1] + jnp.exp2(sink - m)
        ot = pv[:SWA_HEAD_DIM] * (1.0 / l)
        for hh in range(SWA_HEADS_PER_CHAIN):
            lo = ((g * SWA_GROUP) + c * SWA_HEADS_PER_CHAIN + hh) * SWA_HEAD_DIM
            o_ref[0, lo:lo + SWA_HEAD_DIM, sb * SWA_K_BLOCK:(sb + 1) * SWA_K_BLOCK] = (
                ot[:, hh * SWA_K_BLOCK:(hh + 1) * SWA_K_BLOCK].astype(BF16))


def _swa_attn(qt, k, vt, bias, sink):
    b, s = k.shape[0], k.shape[1]
    tq = min(SWA_Q_TILE, s)
    per_tile = tq // SWA_K_BLOCK
    assert per_tile >= 2
    last_tile = s // tq - 1
    bias_block = (1,) + bias.shape[1:]
    return pl.pallas_call(
        _swa_kernel,
        grid=(b, s // tq),
        in_specs=(
            [pl.BlockSpec((1, SWA_KV_HEADS, per_tile, SWA_HEAD_DIM, SWA_GROUP * SWA_K_BLOCK),
                          lambda i, j: (i, 0, j, 0, 0))]
            + [pl.BlockSpec((1,) + k.shape[1:], lambda i, j: (i, 0, 0)),
               pl.BlockSpec((1,) + vt.shape[1:], lambda i, j: (i, 0, 0, 0))]
            + [pl.BlockSpec(bias_block, lambda i, j: (jnp.where(j == 0, 0, 1), 0, 0, 0)),
               pl.BlockSpec(bias_block, lambda i, j: (1, 0, 0, 0)),
               pl.BlockSpec(bias_block, lambda i, j: (jnp.where(j == last_tile, 2, 1), 0, 0, 0)),
               _full(sink.shape)]
        ),
        out_specs=pl.BlockSpec((1, SWA_Q_W, tq), lambda i, j: (i, 0, j)),
        out_shape=jax.ShapeDtypeStruct((b, SWA_Q_W, s), BF16),
        scratch_shapes=[pltpu.VMEM((SWA_SCORE_SLOTS, SWA_BAND_KEYS, SWA_CHAIN_LANES), F32),
                        pltpu.VMEM((SWA_SCORE_SLOTS, 1, SWA_CHAIN_LANES), F32)],
        compiler_params=_params(48 * 2**20, 2),
        name="swa_attn",
    )(qt, k, vt, bias, bias, bias, sink)


def _col(v):
    return v.astype(F32)[:, None]


def _row(v):
    return v.astype(F32)[None, :]


def _prep_layer_a(seq, norm_in, a_w_in, a_q_a_norm, a_w_q_b, a_kv_a_norm, a_w_kv_b, a_q_norm, a_k_norm):
    kr_lo = MLA_Q_RANK + MLA_KV_RANK
    qm_lo = kr_lo + MLA_ROPE
    w_in = _col(norm_in) * a_w_in
    kr_cols = w_in[:, kr_lo:qm_lo]
    wkv = (_col(a_kv_a_norm) * a_w_kv_b).reshape(MLA_KV_RANK, MLA_HEADS, MLA_NOPE + MLA_V)
    inv = 1.0 / (ROPE_THETA ** (np.arange(0, MLA_ROPE, 2, dtype=np.float64) / MLA_ROPE))
    ang = np.arange(seq, dtype=np.float64)[:, None] * inv[None, :]
    cos, sin = np.cos(ang).astype(np.float32), np.sin(ang).astype(np.float32)
    g1 = _col(a_q_norm[MLA_NOPE:MLA_NOPE + MLA_HALF]) * MLA_Q_SCALE
    g2 = _col(a_q_norm[MLA_NOPE + MLA_HALF:]) * MLA_Q_SCALE
    return {
        "w1": jnp.concatenate([w_in[:, :kr_lo], kr_cols, kr_cols], axis=1).astype(BF16),
        "wgt": w_in[:, qm_lo:].T.astype(BF16),
        "wqbt": (_col(a_q_a_norm) * a_w_q_b).T.astype(BF16),
        "wkn": wkv[:, :, :MLA_NOPE].reshape(MLA_KV_RANK, MLA_HEADS * MLA_NOPE).astype(BF16),
        "wvt": wkv[:, :, MLA_NOPE:].reshape(MLA_KV_RANK, MLA_HEADS * MLA_V).T.astype(BF16),
        "qtab": jnp.concatenate([g1 * cos.T, g2 * sin.T, g2 * cos.T, g1 * sin.T], axis=0),
        "gkn": _row(a_k_norm[:MLA_NOPE] * a_q_norm[:MLA_NOPE]),
        "gkr": _row(jnp.tile(a_k_norm[MLA_NOPE:], 2)),
        "ctab": jnp.asarray(np.tile(cos, (1, 4))),
        "stab": jnp.asarray(np.concatenate([-sin, sin, -sin, sin], axis=1)),
    }


def _prep_layer_b(norm_in, b_w_in, b_q_norm, b_k_norm):
    return {
        "wbt": (_col(norm_in) * b_w_in).T.astype(BF16),
        "gk": _row(jnp.tile(b_k_norm * b_q_norm, SWA_KV_HEADS)),
    }


def _prep_mem(mem_norm, w_mem_kv, mem_k_norm, mem_q_norm):
    w = _col(mem_norm) * w_mem_kv
    return {
        "wk": w[:, :MEM_WIDTH].astype(BF16),
        "wvt": w[:, MEM_WIDTH:].T.astype(BF16),
        "gk": _row(mem_k_norm * mem_q_norm),
    }


def _swa_bias():
    r = np.arange(SWA_BAND_KEYS)[:, None]
    c = np.arange(SWA_K_BLOCK)[None, :]
    rel = r - WINDOW - c
    dist = np.abs(rel).astype(np.float64)
    slopes = np.array([_alibi_slope(h) for h in range(SWA_HEADS)]).reshape(SWA_KV_HEADS, SWA_GROUP)
    alibi = -(LOG2E * slopes)[:, None, :, None] * dist[None, :, None, :]
    in_window = (np.abs(rel) <= WINDOW)[None, None, :, None, :]
    key_block = (r // SWA_K_BLOCK)[None, None, :, None, :]
    variant = np.arange(3)[:, None, None, None, None]
    in_seq = ~(((variant == 0) & (key_block == 0)) | ((variant == 2) & (key_block == 2)))
    bias = np.where(in_window & in_seq, alibi[None], NEG).astype(np.float32)
    return jnp.asarray(bias.reshape(3, SWA_KV_HEADS, SWA_BAND_KEYS, SWA_GROUP * SWA_K_BLOCK))


def _swa_sink_lanes(b_sink):
    s = (b_sink.astype(F32) * LOG2E).reshape(SWA_KV_HEADS, 1, SWA_GROUP, 1)
    return jnp.broadcast_to(s, (SWA_KV_HEADS, 1, SWA_GROUP, SWA_K_BLOCK)).reshape(SWA_KV_HEADS, 1, -1)


def _trunk(x, mem, wa, wb, wmem, wots, sink, bias):
    ts = min(SEQ_TILE, x.shape[1])
    b = x.shape[0]
    nt = x.shape[1] // ts

    km, vmt = _memkv(mem, **wmem[0])
    qt, k, vt, qmt, sgt = _proj_a(x, wa, ts)
    mixt = _mla_attn(qt, k, vt)
    mix_spec = pl.BlockSpec((1, MLA_HEADS, 1, MLA_V, ts), lambda i, j: (i, 0, j, 0, 0))
    x = _post(x, mixt, mix_spec, qmt, sgt, km, vmt, wots[0], ts)

    km, vmt = _memkv(mem, **wmem[1])
    qt, k, vt, qmt, sgt = _proj_b(x, wb, min(PROJ_B_TILE, x.shape[1]))
    mixt = _swa_attn(qt, k, vt, bias, sink)
    mix_spec = pl.BlockSpec((1, MIX_WIDTH, ts), lambda i, j: (i, 0, j))
    return _post(x, mixt, mix_spec, qmt, sgt, km, vmt, wots[1], ts)


def kernel(x_prompt, x_sample, mem_prompt, mem_sample, norm_in, w_out, mem_norm, w_mem_kv, mem_q_norm, mem_k_norm, a_w_in, a_q_a_norm, a_w_q_b, a_kv_a_norm, a_w_kv_b, a_q_norm, a_k_norm, b_w_in, b_q_norm, b_k_norm, b_sink):
    assert norm_in.shape[0] == 2 and a_w_in.shape[0] == 1 and b_w_in.shape[0] == 1
    assert x_prompt.shape[1] == x_sample.shape[1]
    seq = x_prompt.shape[1]
    assert seq % min(SWA_Q_TILE, seq) == 0 and seq % min(SEQ_TILE, seq) == 0
    wa = _prep_layer_a(seq, norm_in[0], a_w_in[0], a_q_a_norm[0], a_w_q_b[0], a_kv_a_norm[0], a_w_kv_b[0],
                       a_q_norm[0], a_k_norm[0])
    wb = _prep_layer_b(norm_in[1], b_w_in[0], b_q_norm[0], b_k_norm[0])
    wmem = [_prep_mem(mem_norm[i], w_mem_kv[i], mem_k_norm[i], mem_q_norm[i]) for i in range(2)]
    wots = [w_out[i].astype(BF16) for i in range(2)]
    sink = _swa_sink_lanes(b_sink[0])
    bias = _swa_bias()
    y_prompt = _trunk(x_prompt, mem_prompt, wa, wb, wmem, wots, sink, bias)
    y_sample = _trunk(x_sample, mem_sample, wa, wb, wmem, wots, sink, bias)
    return (y_prompt, y_sample)
```
